```python
import math
import jax, jax.numpy as jnp
from jax import lax
import numpy as np

D_MODEL = 2048
BATCH = 2
SEQ = 8192
DEPTH = 4

MIX_WIDTH = D_MODEL
A_V = 128
A_HEADS = MIX_WIDTH // 2 // A_V
A_NOPE = 128
A_ROPE = 64
A_Q_LORA = 512
A_KV_LORA = 256
A_WIDTH = A_HEADS * A_V
B_HEAD_DIM = 128
B_HEADS = (MIX_WIDTH - A_WIDTH) // B_HEAD_DIM
B_WIDTH = B_HEADS * B_HEAD_DIM
DILATED_PATTERNS = ((128, 1), (512, 4), (2048, 16))
NUM_BUCKETS = 32
MAX_DISTANCE = 1024
ROPE_THETA = 10000.0
Q_BLOCK = 128
EPS = 1e-6
IN_SPLITS = (A_Q_LORA, A_KV_LORA, A_ROPE, A_WIDTH, B_WIDTH, B_WIDTH, B_WIDTH, B_WIDTH)
IN_WIDTH = sum(IN_SPLITS)

kernel_name = "hybrid_mla_dilated_adaln_encoder"


def rms_norm(x, g):
    xf = x.astype(jnp.float32)
    y = xf * lax.rsqrt(jnp.mean(xf * xf, axis=-1, keepdims=True) + EPS)
    return (y * g.astype(jnp.float32)).astype(x.dtype)


def rope_angles(positions):
    inv = 1.0 / (ROPE_THETA ** (jnp.arange(0, A_ROPE, 2, dtype=jnp.float32) / A_ROPE))
    ang = positions.astype(jnp.float32)[..., None] * inv
    return jnp.cos(ang), jnp.sin(ang)


def apply_rope(t, cos, sin):
    tf = t.astype(jnp.float32)
    t1, t2 = jnp.split(tf, 2, axis=-1)
    return jnp.concatenate([t1 * cos - t2 * sin, t2 * cos + t1 * sin], axis=-1).astype(t.dtype)


def mla_attention(qn, qr, kn, kr, v):
    bsz, s_len, h, _ = qn.shape
    nblk = s_len // Q_BLOCK
    scale = 1.0 / math.sqrt(A_NOPE + A_ROPE)

    def to_blocks(t):
        return t.reshape(bsz, nblk, Q_BLOCK, *t.shape[2:]).swapaxes(0, 1)

    def one_block(args):
        qn_b, qr_b = args
        s = (jnp.einsum('bqhd,bkhd->bhqk', qn_b, kn)
             + jnp.einsum('bqhr,bkr->bhqk', qr_b, kr)).astype(jnp.float32) * scale
        p = jax.nn.softmax(s, axis=-1).astype(v.dtype)
        return jnp.einsum('bhqk,bkhd->bqhd', p, v)

    o = lax.map(one_block, (to_blocks(qn), to_blocks(qr)))
    return o.swapaxes(0, 1).reshape(bsz, s_len, h * v.shape[-1])


def dilated_offsets(window, dilation):
    half = window // 2
    return np.arange(-half, half + 1, dilation, dtype=np.int32)


def t5_buckets(rel):
    nb = NUM_BUCKETS // 2
    max_exact = nb // 2
    base = np.where(rel > 0, nb, 0)
    n = np.abs(rel)
    large = max_exact + (np.log(np.maximum(n, 1) / max_exact)
                         / math.log(MAX_DISTANCE / max_exact) * (nb - max_exact)).astype(np.int32)
    large = np.minimum(large, nb - 1)
    return (base + np.where(n < max_exact, n, large)).astype(np.int32)


def dilated_attention(q, k, v, rel_bias):
    bsz, s_len, h, dh = q.shape
    nblk = s_len // Q_BLOCK
    scale = dh ** -0.5
    patterns = []
    for window, dilation in DILATED_PATTERNS:
        off = dilated_offsets(window, dilation)
        bias = rel_bias[t5_buckets(off)].T.astype(jnp.float32)
        patterns.append((jnp.asarray(off), bias))
    q_blocks = q.reshape(bsz, nblk, Q_BLOCK, h, dh).swapaxes(0, 1)
    starts = jnp.arange(nblk, dtype=jnp.int32) * Q_BLOCK

    def one_block(args):
        q_b, start = args
        qpos = start + jnp.arange(Q_BLOCK, dtype=jnp.int32)
        maxes, denoms, outs = [], [], []
        for off, bias in patterns:
            idx = qpos[:, None] + off[None, :]
            valid = (idx >= 0) & (idx < s_len)
            idx = jnp.clip(idx, 0, s_len - 1)
            k_g = jnp.take(k, idx, axis=1)
            v_g = jnp.take(v, idx, axis=1)
            s = jnp.einsum('bqhd,bqjhd->bhqj', q_b, k_g).astype(jnp.float32) * scale
            s = jnp.where(valid[None, None], s + bias[None, :, None, :], -jnp.inf)
            m = jnp.max(s, axis=-1, keepdims=True)
            p = jnp.exp(s - m)
            den = jnp.sum(p, axis=-1, keepdims=True)
            o = jnp.einsum('bhqj,bqjhd->bhqd', (p / den).astype(v.dtype), v_g).astype(jnp.float32)
            maxes.append(m)
            denoms.append(den)
            outs.append(o)
        m_all = jnp.stack(maxes)
        wts = jnp.stack(denoms) * jnp.exp(m_all - jnp.max(m_all, axis=0, keepdims=True))
        out = jnp.sum(wts * jnp.stack(outs), axis=0) / jnp.sum(wts, axis=0)
        return out.transpose(0, 2, 1, 3).astype(q.dtype)

    o = lax.map(one_block, (q_blocks, starts))
    return o.swapaxes(0, 1).reshape(bsz, s_len, h * dh)


def setup_inputs(seed: int = 0) -> dict:
    key = jax.random.key(seed)
    ks = jax.random.split(key, 16)
    f32 = jnp.float32

    def nrm(k, shape, fan_in):
        return jax.random.normal(k, shape, f32) * fan_in ** -0.5

    x = jax.random.normal(ks[0], (BATCH, SEQ, D_MODEL), f32)
    c = jax.random.normal(ks[1], (BATCH, D_MODEL), f32)
    positions = (jnp.arange(SEQ, dtype=jnp.int32)[None, :]
                 + jax.random.randint(ks[2], (BATCH, 1), 0, 1024, dtype=jnp.int32))
    norm_g = 1.0 + 0.02 * jax.random.normal(ks[3], (DEPTH, D_MODEL), f32)
    ada_w = 0.5 * nrm(ks[4], (DEPTH, D_MODEL, 3 * D_MODEL), D_MODEL)
    ada_b = 0.02 * jax.random.normal(ks[5], (DEPTH, 3 * D_MODEL), f32)
    w_in = nrm(ks[6], (DEPTH, D_MODEL, IN_WIDTH), D_MODEL)
    q_a_norm_g = 1.0 + 0.02 * jax.random.normal(ks[7], (DEPTH, A_Q_LORA), f32)
    w_q_up = nrm(ks[8], (DEPTH, A_Q_LORA, A_HEADS * (A_NOPE + A_ROPE)), A_Q_LORA)
    kv_a_norm_g = 1.0 + 0.02 * jax.random.normal(ks[9], (DEPTH, A_KV_LORA), f32)
    w_kv_up = nrm(ks[10], (DEPTH, A_KV_LORA, A_HEADS * (A_NOPE + A_V)), A_KV_LORA)
    rel_bias = 0.5 * jax.random.normal(ks[11], (NUM_BUCKETS, B_HEADS), f32)
    w_out = nrm(ks[12], (DEPTH, MIX_WIDTH, D_MODEL), MIX_WIDTH)
    final_norm_g = 1.0 + 0.02 * jax.random.normal(ks[13], (D_MODEL,), f32)
    return {"x": x, "c": c, "positions": positions, "norm_g": norm_g,
            "ada_w": ada_w, "ada_b": ada_b, "w_in": w_in,
            "q_a_norm_g": q_a_norm_g, "w_q_up": w_q_up,
            "kv_a_norm_g": kv_a_norm_g, "w_kv_up": w_kv_up,
            "rel_bias": rel_bias, "w_out": w_out, "final_norm_g": final_norm_g}


def reference(x, c, positions, norm_g, ada_w, ada_b, w_in, q_a_norm_g, w_q_up,
              kv_a_norm_g, w_kv_up, rel_bias, w_out, final_norm_g):
    bsz, s_len, _ = x.shape
    cos, sin = rope_angles(positions)
    c_act = jax.nn.silu(c)
    split_points = np.cumsum(IN_SPLITS)[:-1].tolist()
    for l in range(DEPTH):
        shift, scale, gate = jnp.split(c_act @ ada_w[l] + ada_b[l], 3, axis=-1)
        h = rms_norm(x, norm_g[l]) * (1.0 + scale[:, None, :]) + shift[:, None, :]
        cq, ckv, k_rope, gate_a, q_b, k_b, v_b, gate_b = jnp.split(h @ w_in[l], split_points, axis=-1)
        q_a = (rms_norm(cq, q_a_norm_g[l]) @ w_q_up[l]).reshape(bsz, s_len, A_HEADS, A_NOPE + A_ROPE)
        q_nope = q_a[..., :A_NOPE]
        q_rope = apply_rope(q_a[..., A_NOPE:], cos[:, :, None, :], sin[:, :, None, :])
        kv_a = (rms_norm(ckv, kv_a_norm_g[l]) @ w_kv_up[l]).reshape(bsz, s_len, A_HEADS, A_NOPE + A_V)
        k_nope, v_a = kv_a[..., :A_NOPE], kv_a[..., A_NOPE:]
        k_rope = apply_rope(k_rope, cos, sin)
        y_a = mla_attention(q_nope, q_rope, k_nope, k_rope, v_a) * jax.nn.silu(gate_a)
        y_b = dilated_attention(q_b.reshape(bsz, s_len, B_HEADS, B_HEAD_DIM),
                                k_b.reshape(bsz, s_len, B_HEADS, B_HEAD_DIM),
                                v_b.reshape(bsz, s_len, B_HEADS, B_HEAD_DIM),
                                rel_bias) * jax.nn.silu(gate_b)
        y = jnp.concatenate([y_a, y_b], axis=-1) @ w_out[l]
        x = x + gate[:, None, :] * y
    return rms_norm(x, final_norm_g)
```

```python
import functools
import math

import numpy as np
import jax
import jax.numpy as jnp
from jax import lax
from jax.experimental import pallas as pl
from jax.experimental.pallas import tpu as pltpu

F32 = jnp.float32
BF16 = jnp.bfloat16

HEAD_DIM = 128
ROPE_DIM = 64
HALF_ROPE = ROPE_DIM // 2
QK_PAD = 256
Q_LORA = 512
KV_LORA = 256
GROUP = 1024
N_GROUPS = 6
PATTERNS = ((128, 1), (512, 4), (2048, 16))
HALF_WIN = 64
NUM_BUCKETS = 32
MAX_DISTANCE = 1024
ROPE_THETA = 10000.0
EPS = 1e-6
NEG = -1e30
LOG2E = math.log2(math.e)

VMEM_LIMIT = 56 * 1024 * 1024

TM_IN = 1024
TS_PREP = 512
TQ_MLA = 512
TK_MLA = 256
TQ_DIL = 2048
TM_OUT = 512
SUB = 128
WIN = 256


def _nt_dot(a, b):
    return lax.dot_general(a, b, (((1,), (1,)), ((), ())), preferred_element_type=F32)


def _rms(x, g):
    return x * lax.rsqrt(jnp.mean(x * x, axis=-1, keepdims=True) + EPS) * g


def _silu(x):
    return x * jax.nn.sigmoid(x)


def _adaln_kernel(c_ref, w_ref, b_ref, o_ref):
    ca = _silu(c_ref[...]).astype(BF16)
    o_ref[...] = jnp.dot(ca, w_ref[...].astype(BF16), preferred_element_type=F32) + b_ref[...]


def _adaln(c8, ada_w, ada_b):
    depth, d, n3 = ada_w.shape
    tn = 768
    return pl.pallas_call(
        _adaln_kernel,
        grid=(depth, n3 // tn),
        in_specs=[
            pl.BlockSpec((8, d), lambda l, j: (0, 0)),
            pl.BlockSpec((None, d, tn), lambda l, j: (l, 0, j)),
            pl.BlockSpec((None, 1, tn), lambda l, j: (l, 0, j)),
        ],
        out_specs=pl.BlockSpec((None, 8, tn), lambda l, j: (l, 0, j)),
        out_shape=jax.ShapeDtypeStruct((depth, 8, n3), F32),
        compiler_params=pltpu.CompilerParams(
            dimension_semantics=("arbitrary", "arbitrary"), vmem_limit_bytes=VMEM_LIMIT),
        name="adaln",
    )(c8, ada_w, ada_b.reshape(depth, 1, n3))


def _in_proj_kernel(x_ref, mod_ref, g_ref, w_ref, o_ref, h_sc):
    @pl.when(pl.program_id(1) == 0)
    def _():
        y = _rms(x_ref[...], g_ref[...])
        h = y * (1.0 + mod_ref[1:2, :]) + mod_ref[0:1, :]
        h_sc[...] = h.astype(BF16)

    o_ref[...] = jnp.dot(h_sc[...], w_ref[...], preferred_element_type=F32)


def _in_proj(x2, mod_l, g_l, w_l, seq):
    rows, d = x2.shape
    tm = TM_IN
    per_b = seq // tm
    return pl.pallas_call(
        _in_proj_kernel,
        grid=(rows // tm, N_GROUPS),
        in_specs=[
            pl.BlockSpec((tm, d), lambda i, j: (i, 0)),
            pl.BlockSpec((None, 3, d), lambda i, j: (i // per_b, 0, 0)),
            pl.BlockSpec((1, d), lambda i, j: (0, 0)),
            pl.BlockSpec((d, GROUP), lambda i, j: (0, j)),
        ],
        out_specs=pl.BlockSpec((None, tm, GROUP), lambda i, j: (j, i, 0)),
        out_shape=jax.ShapeDtypeStruct((N_GROUPS, rows, GROUP), F32),
        scratch_shapes=[pltpu.VMEM((tm, d), BF16)],
        compiler_params=pltpu.CompilerParams(
            dimension_semantics=("arbitrary", "arbitrary"), vmem_limit_bytes=VMEM_LIMIT),
        name="in_proj",
    )(x2, mod_l, g_l, w_l)


def _mla_prep_kernel(lat_ref, cos_ref, sin_ref, ctab_ref, qg_ref, kvg_ref,
                     wqt_ref, wk_ref, wvt_ref, qt_ref, k_ref, vt_ref, *, heads):
    ts = lat_ref.shape[0]
    cqn = _rms(lat_ref[:, 0:Q_LORA], qg_ref[...]).astype(BF16)
    ckvn = _rms(lat_ref[:, Q_LORA:Q_LORA + KV_LORA], kvg_ref[...]).astype(BF16)

    qt = _nt_dot(wqt_ref[...], cqn)
    cos_t = cos_ref[...]
    sin_t = sin_ref[...]
    for h in range(heads):
        base = h * QK_PAD
        qt_ref[h, 0:HEAD_DIM, :] = qt[base:base + HEAD_DIM].astype(BF16)
        t1 = qt[base + HEAD_DIM:base + HEAD_DIM + HALF_ROPE]
        t2 = qt[base + HEAD_DIM + HALF_ROPE:base + HEAD_DIM + ROPE_DIM]
        qt_ref[h, HEAD_DIM:HEAD_DIM + HALF_ROPE, :] = (t1 * cos_t - t2 * sin_t).astype(BF16)
        qt_ref[h, HEAD_DIM + HALF_ROPE:HEAD_DIM + ROPE_DIM, :] = (t2 * cos_t + t1 * sin_t).astype(BF16)
        qt_ref[h, HEAD_DIM + ROPE_DIM:QK_PAD, :] = jnp.zeros((QK_PAD - HEAD_DIM - ROPE_DIM, ts), BF16)

    y = lat_ref[:, Q_LORA + KV_LORA:Q_LORA + KV_LORA + 128] * ctab_ref[...]
    folded = y + pltpu.roll(y, 64, axis=1)
    lane = lax.broadcasted_iota(jnp.int32, folded.shape, 1)
    k_rope = jnp.where(lane < ROPE_DIM, folded, 0.0).astype(BF16)

    kn = jnp.dot(ckvn, wk_ref[...], preferred_element_type=F32)
    vt = _nt_dot(wvt_ref[...], ckvn)
    for h in range(heads):
        k_ref[h, :, 0:HEAD_DIM] = kn[:, h * HEAD_DIM:(h + 1) * HEAD_DIM].astype(BF16)
        k_ref[h, :, HEAD_DIM:QK_PAD] = k_rope
        for j in range(ts // TK_MLA):
            vt_ref[h, j] = vt[h * HEAD_DIM:(h + 1) * HEAD_DIM,
                              j * TK_MLA:(j + 1) * TK_MLA].astype(BF16)


def _mla_prep(planes, cos_t, sin_t, ctab, qg, kvg, wqt, wk, wvt, bsz, seq, heads):
    ts = TS_PREP
    per_b = seq // ts
    kern = functools.partial(_mla_prep_kernel, heads=heads)
    return pl.pallas_call(
        kern,
        grid=(bsz, per_b),
        in_specs=[
            pl.BlockSpec((None, ts, GROUP), lambda b, i: (0, b * per_b + i, 0)),
            pl.BlockSpec((None, HALF_ROPE, ts), lambda b, i: (b, 0, i)),
            pl.BlockSpec((None, HALF_ROPE, ts), lambda b, i: (b, 0, i)),
            pl.BlockSpec((None, ts, 128), lambda b, i: (b, i, 0)),
            pl.BlockSpec((1, Q_LORA), lambda b, i: (0, 0)),
            pl.BlockSpec((1, KV_LORA), lambda b, i: (0, 0)),
            pl.BlockSpec(wqt.shape, lambda b, i: (0, 0)),
            pl.BlockSpec(wk.shape, lambda b, i: (0, 0)),
            pl.BlockSpec(wvt.shape, lambda b, i: (0, 0)),
        ],
        out_specs=[
            pl.BlockSpec((None, heads, QK_PAD, ts), lambda b, i: (b, 0, 0, i)),
            pl.BlockSpec((None, heads, ts, QK_PAD), lambda b, i: (b, 0, i, 0)),
            pl.BlockSpec((None, heads, ts // TK_MLA, HEAD_DIM, TK_MLA), lambda b, i: (b, 0, i, 0, 0)),
        ],
        out_shape=[
            jax.ShapeDtypeStruct((bsz, heads, QK_PAD, seq), BF16),
            jax.ShapeDtypeStruct((bsz, heads, seq, QK_PAD), BF16),
            jax.ShapeDtypeStruct((bsz, heads, seq // TK_MLA, HEAD_DIM, TK_MLA), BF16),
        ],
        compiler_params=pltpu.CompilerParams(
            dimension_semantics=("arbitrary", "arbitrary"), vmem_limit_bytes=VMEM_LIMIT),
        name="mla_prep",
    )(planes, cos_t, sin_t, ctab, qg, kvg, wqt, wk, wvt)


def _mla_attn_kernel(qt_ref, k_ref, vt_ref, g_ref, o_ref, m_sc, l_sc, acc_sc, *, exp_scale):
    n_kt = vt_ref.shape[0]
    qt = qt_ref[...]
    m_sc[...] = jnp.full(m_sc.shape, -jnp.inf, F32)
    l_sc[...] = jnp.zeros(l_sc.shape, F32)
    acc_sc[...] = jnp.zeros(acc_sc.shape, F32)

    def body(kt, carry):
        k = k_ref[pl.ds(pl.multiple_of(kt * TK_MLA, TK_MLA), TK_MLA), :]
        s = jnp.dot(k, qt, preferred_element_type=F32)
        m_prev = m_sc[...]
        m_new = jnp.maximum(m_prev, jnp.max(s, axis=0, keepdims=True))
        alpha = jnp.exp2((m_prev - m_new) * exp_scale)
        p = jnp.exp2((s - m_new) * exp_scale)
        l_sc[...] = alpha * l_sc[...] + jnp.sum(p, axis=0, keepdims=True)
        acc_sc[...] = alpha * acc_sc[...] + jnp.dot(vt_ref[kt], p.astype(BF16),
                                                    preferred_element_type=F32)
        m_sc[...] = m_new
        return carry

    lax.fori_loop(0, n_kt, body, 0)
    o_t = acc_sc[...] / l_sc[...]
    o_ref[...] = (o_t.T * _silu(g_ref[...])).astype(BF16)


def _mla_attn(qt, k, vt, planes, bsz, seq, heads):
    tq = TQ_MLA
    per_b = seq // tq
    n_kt = seq // TK_MLA
    exp_scale = LOG2E / math.sqrt(HEAD_DIM + ROPE_DIM)
    kern = functools.partial(_mla_attn_kernel, exp_scale=exp_scale)
    return pl.pallas_call(
        kern,
        grid=(bsz, heads, per_b),
        in_specs=[
            pl.BlockSpec((None, None, QK_PAD, tq), lambda b, h, i: (b, h, 0, i)),
            pl.BlockSpec((None, None, seq, QK_PAD), lambda b, h, i: (b, h, 0, 0)),
            pl.BlockSpec((None, None, n_kt, HEAD_DIM, TK_MLA), lambda b, h, i: (b, h, 0, 0, 0)),
            pl.BlockSpec((None, tq, HEAD_DIM), lambda b, h, i: (1, b * per_b + i, h)),
        ],
        out_specs=pl.BlockSpec((tq, HEAD_DIM), lambda b, h, i: (b * per_b + i, h)),
        out_shape=jax.ShapeDtypeStruct((bsz * seq, heads * HEAD_DIM), BF16),
        scratch_shapes=[pltpu.VMEM((1, tq), F32), pltpu.VMEM((1, tq), F32),
                        pltpu.VMEM((HEAD_DIM, tq), F32)],
        compiler_params=pltpu.CompilerParams(
            dimension_semantics=("arbitrary", "arbitrary", "arbitrary"),
            vmem_limit_bytes=VMEM_LIMIT),
        name="mla_attn",
    )(qt, k, vt, planes)


def _dilated_kernel(q_ref, k_ref, v_ref, g_ref, tbl_ref, o_ref,
                    kc_sc, vc_sc, o_sc, m_sc, l_sc, *, seq, scale):
    tq = q_ref.shape[0]
    step = pl.program_id(2)

    @pl.when(step == 0)
    def _():
        chunk = 256
        for g, (_, d) in enumerate(PATTERNS):
            cls_len = seq // d
            for r in range(d):
                def copy(c, carry, g=g, d=d, r=r, cls_len=cls_len):
                    src = pl.ds(r + d * chunk * c, chunk, stride=d) if d > 1 else pl.ds(chunk * c, chunk)
                    dst = pl.ds(pl.multiple_of(r * cls_len + chunk * c, chunk), chunk)
                    kc_sc[g, dst, :] = k_ref[src, :].astype(BF16)
                    vc_sc[g, dst, :] = v_ref[src, :].astype(BF16)
                    return carry
                lax.fori_loop(0, cls_len // chunk, copy, 0)

    t0 = step * tq
    for g, (_, d) in enumerate(PATTERNS):
        cls_len = seq // d
        for r in range(d):
            for sb in range(tq // d // SUB):
                rows = pl.ds(r + d * sb * SUB, SUB, stride=d) if d > 1 else pl.ds(sb * SUB, SUB)
                qc = q_ref[rows, :].astype(BF16)
                i0 = t0 // d + sb * SUB
                ws = jnp.clip(i0 - HALF_WIN, 0, cls_len - WIN)
                var = (i0 - ws) // HALF_WIN
                win = pl.ds(pl.multiple_of(r * cls_len + ws, HALF_WIN), WIN)
                s = _nt_dot(qc, kc_sc[g, win, :]) * scale + tbl_ref[g, var]
                m = jnp.max(s, axis=-1, keepdims=True)
                p = jnp.exp(s - m)
                den = jnp.sum(p, axis=-1, keepdims=True)
                o = jnp.dot((p / den).astype(BF16), vc_sc[g, win, :], preferred_element_type=F32)
                o_sc[g, rows, :] = o
                m_sc[g, rows, :] = jnp.broadcast_to(m, (SUB, HEAD_DIM))
                l_sc[g, rows, :] = jnp.broadcast_to(den, (SUB, HEAD_DIM))

    m_all = jnp.maximum(jnp.maximum(m_sc[0], m_sc[1]), m_sc[2])
    num = jnp.zeros((tq, HEAD_DIM), F32)
    tot = jnp.zeros((tq, HEAD_DIM), F32)
    for g in range(len(PATTERNS)):
        w = l_sc[g] * jnp.exp(m_sc[g] - m_all)
        num = num + w * o_sc[g]
        tot = tot + w
    o_ref[...] = ((num / tot) * _silu(g_ref[...])).astype(BF16)


def _dilated_attn(planes, tbl, bsz, seq, heads):
    tq = TQ_DIL
    per_b = seq // tq
    n_pat = len(PATTERNS)
    kern = functools.partial(_dilated_kernel, seq=seq, scale=HEAD_DIM ** -0.5)
    return pl.pallas_call(
        kern,
        grid=(bsz, heads, per_b),
        in_specs=[
            pl.BlockSpec((None, tq, HEAD_DIM), lambda b, h, i: (2, b * per_b + i, h)),
            pl.BlockSpec((None, seq, HEAD_DIM), lambda b, h, i: (3, b, h)),
            pl.BlockSpec((None, seq, HEAD_DIM), lambda b, h, i: (4, b, h)),
            pl.BlockSpec((None, tq, HEAD_DIM), lambda b, h, i: (5, b * per_b + i, h)),
            pl.BlockSpec((None, n_pat, 3, SUB, WIN), lambda b, h, i: (h, 0, 0, 0, 0)),
        ],
        out_specs=pl.BlockSpec((tq, HEAD_DIM), lambda b, h, i: (b * per_b + i, h)),
        out_shape=jax.ShapeDtypeStruct((bsz * seq, heads * HEAD_DIM), BF16),
        scratch_shapes=[
            pltpu.VMEM((n_pat, seq, HEAD_DIM), BF16),
            pltpu.VMEM((n_pat, seq, HEAD_DIM), BF16),
            pltpu.VMEM((n_pat, tq, HEAD_DIM), F32),
            pltpu.VMEM((n_pat, tq, HEAD_DIM), F32),
            pltpu.VMEM((n_pat, tq, HEAD_DIM), F32),
        ],
        compiler_params=pltpu.CompilerParams(
            dimension_semantics=("arbitrary", "arbitrary", "arbitrary"),
            vmem_limit_bytes=VMEM_LIMIT),
        name="dilated_attn",
    )(planes, planes, planes, planes, tbl)


def _out_proj_kernel(ya_ref, yb_ref, x_ref, mod_ref, w_ref, fg_ref, o_ref, *, final):
    half = ya_ref.shape[1]
    y = (jnp.dot(ya_ref[...], w_ref[0:half, :], preferred_element_type=F32)
         + jnp.dot(yb_ref[...], w_ref[half:, :], preferred_element_type=F32))
    xn = x_ref[...] + mod_ref[2:3, :] * y
    if final:
        xn = _rms(xn, fg_ref[...])
    o_ref[...] = xn


def _out_proj(ya, yb, x2, mod_l, w_l, fg, seq, final):
    rows, d = x2.shape
    tm = TM_OUT
    per_b = seq // tm
    half = ya.shape[1]
    kern = functools.partial(_out_proj_kernel, final=final)
    return pl.pallas_call(
        kern,
        grid=(rows // tm,),
        in_specs=[
            pl.BlockSpec((tm, half), lambda i: (i, 0)),
            pl.BlockSpec((tm, half), lambda i: (i, 0)),
            pl.BlockSpec((tm, d), lambda i: (i, 0)),
            pl.BlockSpec((None, 3, d), lambda i: (i // per_b, 0, 0)),
            pl.BlockSpec(w_l.shape, lambda i: (0, 0)),
            pl.BlockSpec((1, d), lambda i: (0, 0)),
        ],
        out_specs=pl.BlockSpec((tm, d), lambda i: (i, 0)),
        out_shape=jax.ShapeDtypeStruct((rows, d), F32),
        input_output_aliases={2: 0},
        compiler_params=pltpu.CompilerParams(
            dimension_semantics=("arbitrary",), vmem_limit_bytes=VMEM_LIMIT),
        name="out_proj",
    )(ya, yb, x2, mod_l, w_l, fg)


def _t5_buckets(rel):
    nb = NUM_BUCKETS // 2
    max_exact = nb // 2
    base = np.where(rel > 0, nb, 0)
    n = np.abs(rel)
    large = max_exact + (np.log(np.maximum(n, 1) / max_exact)
                         / math.log(MAX_DISTANCE / max_exact) * (nb - max_exact)).astype(np.int32)
    large = np.minimum(large, nb - 1)
    return (base + np.where(n < max_exact, n, large)).astype(np.int32)


def _bias_tables(rel_bias):
    qi = np.arange(SUB)[:, None]
    kj = np.arange(WIN)[None, :]
    tabs = []
    for _, d in PATTERNS:
        per_var = []
        for var in range(3):
            j = kj - qi - var * HALF_WIN
            band = np.abs(j) <= HALF_WIN
            buckets = _t5_buckets(np.where(band, j, 0) * d)
            vals = jnp.take(rel_bias, jnp.asarray(buckets.reshape(-1)), axis=0)
            vals = vals.reshape(SUB, WIN, -1)
            per_var.append(jnp.where(jnp.asarray(band)[:, :, None], vals, NEG))
        tabs.append(jnp.stack(per_var))
    return jnp.stack(tabs).transpose(4, 0, 1, 2, 3).astype(F32)


def _rope_tables(positions):
    inv = 1.0 / (ROPE_THETA ** (jnp.arange(0, ROPE_DIM, 2, dtype=F32) / ROPE_DIM))
    ang = positions.astype(F32)[..., None] * inv
    cos, sin = jnp.cos(ang), jnp.sin(ang)
    ctab = jnp.concatenate([cos, cos, -sin, sin], axis=-1)
    return cos.swapaxes(1, 2), sin.swapaxes(1, 2), ctab


def kernel(x, c, positions, norm_g, ada_w, ada_b, w_in, q_a_norm_g, w_q_up, kv_a_norm_g,
           w_kv_up, rel_bias, w_out, final_norm_g):
    bsz, seq, d = x.shape
    depth = w_in.shape[0]
    heads = w_q_up.shape[2] // (HEAD_DIM + ROPE_DIM)
    width = heads * HEAD_DIM

    cq_w = w_in[:, :, 0:Q_LORA]
    ckv_w = w_in[:, :, Q_LORA:Q_LORA + KV_LORA]
    kr_w = w_in[:, :, Q_LORA + KV_LORA:Q_LORA + KV_LORA + ROPE_DIM]
    k1, k2 = kr_w[..., :HALF_ROPE], kr_w[..., HALF_ROPE:]
    rest = w_in[:, :, Q_LORA + KV_LORA + ROPE_DIM:]
    pad = jnp.zeros((depth, d, GROUP - Q_LORA - KV_LORA - 2 * ROPE_DIM), w_in.dtype)
    w_in_g = jnp.concatenate([cq_w, ckv_w, k1, k2, k2, k1, pad, rest], axis=-1).astype(BF16)

    wq = w_q_up.reshape(depth, Q_LORA, heads, HEAD_DIM + ROPE_DIM)
    wq = jnp.pad(wq, ((0, 0), (0, 0), (0, 0), (0, QK_PAD - HEAD_DIM - ROPE_DIM)))
    wqt = wq.reshape(depth, Q_LORA, heads * QK_PAD).swapaxes(1, 2).astype(BF16)
    wkv = w_kv_up.reshape(depth, KV_LORA, heads, 2 * HEAD_DIM)
    wk = wkv[..., :HEAD_DIM].reshape(depth, KV_LORA, width).astype(BF16)
    wvt = wkv[..., HEAD_DIM:].reshape(depth, KV_LORA, width).swapaxes(1, 2).astype(BF16)
    w_out_b = w_out.astype(BF16)

    cos_t, sin_t, ctab = _rope_tables(positions)
    tbl = _bias_tables(rel_bias)

    c8 = jnp.pad(c, ((0, 8 - bsz), (0, 0)))
    mod = _adaln(c8, ada_w, ada_b)[:, :bsz].reshape(depth, bsz, 3, d)

    x2 = x.reshape(bsz * seq, d)
    fg = final_norm_g.reshape(1, d)
    for l in range(depth):
        planes = _in_proj(x2, mod[l], norm_g[l].reshape(1, d), w_in_g[l], seq)
        qt, k, vt = _mla_prep(planes, cos_t, sin_t, ctab, q_a_norm_g[l].reshape(1, Q_LORA),
                              kv_a_norm_g[l].reshape(1, KV_LORA), wqt[l], wk[l], wvt[l],
                              bsz, seq, heads)
        ya = _mla_attn(qt, k, vt, planes, bsz, seq, heads)
        yb = _dilated_attn(planes, tbl, bsz, seq, heads)
        x2 = _out_proj(ya, yb, x2, mod[l], w_out_b[l], fg, seq, final=(l == depth - 1))
    return x2.reshape(bsz, seq, d)
```

```python
import functools
import math

import numpy as np
import jax
import jax.numpy as jnp
from jax import lax
from jax.experimental import pallas as pl
from jax.experimental.pallas import tpu as pltpu

F32 = jnp.float32
BF16 = jnp.bfloat16

HEAD_DIM = 128
ROPE_DIM = 64
HALF_ROPE = ROPE_DIM // 2
QK_PAD = 256
V_ROWS = HEAD_DIM + 16
Q_LORA = 512
KV_LORA = 256
GROUP = 1024
N_GROUPS = 6
PATTERNS = ((128, 1), (512, 4), (2048, 16))
HALF_WIN = 64
NUM_BUCKETS = 32
MAX_DISTANCE = 1024
ROPE_THETA = 10000.0
EPS = 1e-6
NEG = -1e30
LOG2E = math.log2(math.e)

VMEM_LIMIT = 56 * 1024 * 1024

TM_IN = 1024
TS_PREP = 512
TQ_MLA = 512
TK_MLA = 512
TQ_DIL = 2048
TM_OUT = 512
SUB = 128
WIN = 256


def _nt_dot(a, b):
    return lax.dot_general(a, b, (((1,), (1,)), ((), ())), preferred_element_type=F32)


def _rms(x, g):
    return x * lax.rsqrt(jnp.mean(x * x, axis=-1, keepdims=True) + EPS) * g


def _silu(x):
    return x * jax.nn.sigmoid(x)


def _adaln_kernel(c_ref, w_ref, b_ref, o_ref):
    ca = _silu(c_ref[...]).astype(BF16)
    o_ref[...] = jnp.dot(ca, w_ref[...].astype(BF16), preferred_element_type=F32) + b_ref[...]


def _adaln(c8, ada_w, ada_b):
    depth, d, n3 = ada_w.shape
    tn = 768
    return pl.pallas_call(
        _adaln_kernel,
        grid=(depth, n3 // tn),
        in_specs=[
            pl.BlockSpec((8, d), lambda l, j: (0, 0)),
            pl.BlockSpec((None, d, tn), lambda l, j: (l, 0, j)),
            pl.BlockSpec((None, 1, tn), lambda l, j: (l, 0, j)),
        ],
        out_specs=pl.BlockSpec((None, 8, tn), lambda l, j: (l, 0, j)),
        out_shape=jax.ShapeDtypeStruct((depth, 8, n3), F32),
        compiler_params=pltpu.CompilerParams(
            dimension_semantics=("arbitrary", "arbitrary"), vmem_limit_bytes=VMEM_LIMIT),
        name="adaln",
    )(c8, ada_w, ada_b.reshape(depth, 1, n3))


def _in_proj_kernel(x_ref, mod_ref, g_ref, w_ref, o_ref, h_sc):
    @pl.when(pl.program_id(1) == 0)
    def _():
        y = _rms(x_ref[...], g_ref[...])
        h = y * (1.0 + mod_ref[1:2, :]) + mod_ref[0:1, :]
        h_sc[...] = h.astype(BF16)

    o_ref[...] = jnp.dot(h_sc[...], w_ref[...], preferred_element_type=F32)


def _in_proj(x2, mod_l, g_l, w_l, seq):
    rows, d = x2.shape
    tm = TM_IN
    per_b = seq // tm
    return pl.pallas_call(
        _in_proj_kernel,
        grid=(rows // tm, N_GROUPS),
        in_specs=[
            pl.BlockSpec((tm, d), lambda i, j: (i, 0)),
            pl.BlockSpec((None, 3, d), lambda i, j: (i // per_b, 0, 0)),
            pl.BlockSpec((1, d), lambda i, j: (0, 0)),
            pl.BlockSpec((d, GROUP), lambda i, j: (0, j)),
        ],
        out_specs=pl.BlockSpec((None, tm, GROUP), lambda i, j: (j, i, 0)),
        out_shape=jax.ShapeDtypeStruct((N_GROUPS, rows, GROUP), F32),
        scratch_shapes=[pltpu.VMEM((tm, d), BF16)],
        compiler_params=pltpu.CompilerParams(
            dimension_semantics=("arbitrary", "arbitrary"), vmem_limit_bytes=VMEM_LIMIT),
        name="in_proj",
    )(x2, mod_l, g_l, w_l)


def _mla_prep_kernel(lat_ref, cos_ref, sin_ref, ctab_ref, qg_ref, kvg_ref,
                     wqt_ref, wk_ref, wvt_ref, qt_ref, k_ref, vt_ref, *, heads):
    ts = lat_ref.shape[0]
    cqn = _rms(lat_ref[:, 0:Q_LORA], qg_ref[...]).astype(BF16)
    ckvn = _rms(lat_ref[:, Q_LORA:Q_LORA + KV_LORA], kvg_ref[...]).astype(BF16)

    qt = _nt_dot(wqt_ref[...], cqn)
    cos_t = cos_ref[...]
    sin_t = sin_ref[...]
    for h in range(heads):
        base = h * QK_PAD
        qt_ref[h, 0:HEAD_DIM, :] = qt[base:base + HEAD_DIM].astype(BF16)
        t1 = qt[base + HEAD_DIM:base + HEAD_DIM + HALF_ROPE]
        t2 = qt[base + HEAD_DIM + HALF_ROPE:base + HEAD_DIM + ROPE_DIM]
        qt_ref[h, HEAD_DIM:HEAD_DIM + HALF_ROPE, :] = (t1 * cos_t - t2 * sin_t).astype(BF16)
        qt_ref[h, HEAD_DIM + HALF_ROPE:HEAD_DIM + ROPE_DIM, :] = (t2 * cos_t + t1 * sin_t).astype(BF16)
        qt_ref[h, HEAD_DIM + ROPE_DIM:QK_PAD, :] = jnp.zeros((QK_PAD - HEAD_DIM - ROPE_DIM, ts), BF16)

    y = lat_ref[:, Q_LORA + KV_LORA:Q_LORA + KV_LORA + 128] * ctab_ref[...]
    folded = y + pltpu.roll(y, 64, axis=1)
    lane = lax.broadcasted_iota(jnp.int32, folded.shape, 1)
    k_rope = jnp.where(lane < ROPE_DIM, folded, 0.0).astype(BF16)

    kn = jnp.dot(ckvn, wk_ref[...], preferred_element_type=F32)
    vt = _nt_dot(wvt_ref[...], ckvn)
    for h in range(heads):
        k_ref[h, :, 0:HEAD_DIM] = kn[:, h * HEAD_DIM:(h + 1) * HEAD_DIM].astype(BF16)
        k_ref[h, :, HEAD_DIM:QK_PAD] = k_rope
        for j in range(ts // TK_MLA):
            vt_ref[h, j, 0:HEAD_DIM, :] = vt[h * HEAD_DIM:(h + 1) * HEAD_DIM,
                                             j * TK_MLA:(j + 1) * TK_MLA].astype(BF16)
            vt_ref[h, j, HEAD_DIM:V_ROWS, :] = jnp.ones((V_ROWS - HEAD_DIM, TK_MLA), BF16)


def _mla_prep(planes, cos_t, sin_t, ctab, qg, kvg, wqt, wk, wvt, bsz, seq, heads):
    ts = TS_PREP
    per_b = seq // ts
    kern = functools.partial(_mla_prep_kernel, heads=heads)
    return pl.pallas_call(
        kern,
        grid=(bsz, per_b),
        in_specs=[
            pl.BlockSpec((None, ts, GROUP), lambda b, i: (0, b * per_b + i, 0)),
            pl.BlockSpec((None, HALF_ROPE, ts), lambda b, i: (b, 0, i)),
            pl.BlockSpec((None, HALF_ROPE, ts), lambda b, i: (b, 0, i)),
            pl.BlockSpec((None, ts, 128), lambda b, i: (b, i, 0)),
            pl.BlockSpec((1, Q_LORA), lambda b, i: (0, 0)),
            pl.BlockSpec((1, KV_LORA), lambda b, i: (0, 0)),
            pl.BlockSpec(wqt.shape, lambda b, i: (0, 0)),
            pl.BlockSpec(wk.shape, lambda b, i: (0, 0)),
            pl.BlockSpec(wvt.shape, lambda b, i: (0, 0)),
        ],
        out_specs=[
            pl.BlockSpec((None, heads, QK_PAD, ts), lambda b, i: (b, 0, 0, i)),
            pl.BlockSpec((None, heads, ts, QK_PAD), lambda b, i: (b, 0, i, 0)),
            pl.BlockSpec((None, heads, ts // TK_MLA, V_ROWS, TK_MLA), lambda b, i: (b, 0, i, 0, 0)),
        ],
        out_shape=[
            jax.ShapeDtypeStruct((bsz, heads, QK_PAD, seq), BF16),
            jax.ShapeDtypeStruct((bsz, heads, seq, QK_PAD), BF16),
            jax.ShapeDtypeStruct((bsz, heads, seq // TK_MLA, V_ROWS, TK_MLA), BF16),
        ],
        compiler_params=pltpu.CompilerParams(
            dimension_semantics=("arbitrary", "arbitrary"), vmem_limit_bytes=VMEM_LIMIT),
        name="mla_prep",
    )(planes, cos_t, sin_t, ctab, qg, kvg, wqt, wk, wvt)


def _mla_attn_kernel(qt_ref, k_ref, vt_ref, g_ref, o_ref, s_sc, m_sc, acc_sc, *, exp_scale):
    n_kt = vt_ref.shape[0]
    m_sc[...] = jnp.full(m_sc.shape, -jnp.inf, F32)
    acc_sc[...] = jnp.zeros(acc_sc.shape, F32)

    def scores(kt):
        k = k_ref[pl.ds(pl.multiple_of(kt * TK_MLA, TK_MLA), TK_MLA), :]
        return jnp.dot(k, qt_ref[...], preferred_element_type=F32)

    def step(kt, slot):
        s_sc[1 - slot] = scores(jnp.minimum(kt + 1, n_kt - 1))
        s = s_sc[slot]
        m_prev = m_sc[...]
        m_new = jnp.maximum(m_prev, jnp.max(s, axis=0, keepdims=True))
        alpha = jnp.exp2((m_prev - m_new) * exp_scale)
        p = jnp.exp2((s - m_new) * exp_scale)
        acc_sc[...] = alpha * acc_sc[...] + jnp.dot(vt_ref[kt], p.astype(BF16),
                                                    preferred_element_type=F32)
        m_sc[...] = m_new

    s_sc[0] = scores(0)

    def body(i, carry):
        step(2 * i, 0)
        step(2 * i + 1, 1)
        return carry

    lax.fori_loop(0, n_kt // 2, body, 0)
    o_t = acc_sc[0:HEAD_DIM, :] / acc_sc[HEAD_DIM:HEAD_DIM + 1, :]
    o_ref[...] = (o_t.T * _silu(g_ref[...])).astype(BF16)


def _mla_attn(qt, k, vt, planes, bsz, seq, heads):
    tq = TQ_MLA
    per_b = seq // tq
    n_kt = seq // TK_MLA
    exp_scale = LOG2E / math.sqrt(HEAD_DIM + ROPE_DIM)
    kern = functools.partial(_mla_attn_kernel, exp_scale=exp_scale)
    return pl.pallas_call(
        kern,
        grid=(bsz, heads, per_b),
        in_specs=[
            pl.BlockSpec((None, None, QK_PAD, tq), lambda b, h, i: (b, h, 0, i)),
            pl.BlockSpec((None, None, seq, QK_PAD), lambda b, h, i: (b, h, 0, 0)),
            pl.BlockSpec((None, None, n_kt, V_ROWS, TK_MLA), lambda b, h, i: (b, h, 0, 0, 0)),
            pl.BlockSpec((None, tq, HEAD_DIM), lambda b, h, i: (1, b * per_b + i, h)),
        ],
        out_specs=pl.BlockSpec((tq, HEAD_DIM), lambda b, h, i: (b * per_b + i, h)),
        out_shape=jax.ShapeDtypeStruct((bsz * seq, heads * HEAD_DIM), BF16),
        scratch_shapes=[pltpu.VMEM((2, TK_MLA, tq), F32), pltpu.VMEM((1, tq), F32),
                        pltpu.VMEM((V_ROWS, tq), F32)],
        compiler_params=pltpu.CompilerParams(
            dimension_semantics=("arbitrary", "arbitrary", "arbitrary"),
            vmem_limit_bytes=VMEM_LIMIT),
        name="mla_attn",
    )(qt, k, vt, planes)


def _dilated_kernel(q_ref, k_ref, v_ref, g_ref, tbl_ref, o_ref,
                    kc_sc, vc_sc, o_sc, m_sc, l_sc, *, seq, scale):
    tq = q_ref.shape[0]
    step = pl.program_id(2)

    @pl.when(step == 0)
    def _():
        chunk = 256
        for g, (_, d) in enumerate(PATTERNS):
            cls_len = seq // d
            for r in range(d):
                def copy(c, carry, g=g, d=d, r=r, cls_len=cls_len):
                    src = pl.ds(r + d * chunk * c, chunk, stride=d) if d > 1 else pl.ds(chunk * c, chunk)
                    dst = pl.ds(pl.multiple_of(r * cls_len + chunk * c, chunk), chunk)
                    kc_sc[g, dst, :] = k_ref[src, :].astype(BF16)
                    vc_sc[g, dst, :] = v_ref[src, :].astype(BF16)
                    return carry
                lax.fori_loop(0, cls_len // chunk, copy, 0)

    t0 = step * tq
    for g, (_, d) in enumerate(PATTERNS):
        cls_len = seq // d
        for r in range(d):
            for sb in range(tq // d // SUB):
                rows = pl.ds(r + d * sb * SUB, SUB, stride=d) if d > 1 else pl.ds(sb * SUB, SUB)
                qc = q_ref[rows, :].astype(BF16)
                i0 = t0 // d + sb * SUB
                ws = jnp.clip(i0 - HALF_WIN, 0, cls_len - WIN)
                var = (i0 - ws) // HALF_WIN
                win = pl.ds(pl.multiple_of(r * cls_len + ws, HALF_WIN), WIN)
                s = _nt_dot(qc, kc_sc[g, win, :]) * scale + tbl_ref[g, var]
                m = jnp.max(s, axis=-1, keepdims=True)
                p = jnp.exp(s - m)
                den = jnp.sum(p, axis=-1, keepdims=True)
                o = jnp.dot((p / den).astype(BF16), vc_sc[g, win, :], preferred_element_type=F32)
                o_sc[g, rows, :] = o
                m_sc[g, rows, :] = jnp.broadcast_to(m, (SUB, HEAD_DIM))
                l_sc[g, rows, :] = jnp.broadcast_to(den, (SUB, HEAD_DIM))

    m_all = jnp.maximum(jnp.maximum(m_sc[0], m_sc[1]), m_sc[2])
    num = jnp.zeros((tq, HEAD_DIM), F32)
    tot = jnp.zeros((tq, HEAD_DIM), F32)
    for g in range(len(PATTERNS)):
        w = l_sc[g] * jnp.exp(m_sc[g] - m_all)
        num = num + w * o_sc[g]
        tot = tot + w
    o_ref[...] = ((num / tot) * _silu(g_ref[...])).astype(BF16)


def _dilated_attn(planes, tbl, bsz, seq, heads):
    tq = TQ_DIL
    per_b = seq // tq
    n_pat = len(PATTERNS)
    kern = functools.partial(_dilated_kernel, seq=seq, scale=HEAD_DIM ** -0.5)
    return pl.pallas_call(
        kern,
        grid=(bsz, heads, per_b),
        in_specs=[
            pl.BlockSpec((None, tq, HEAD_DIM), lambda b, h, i: (2, b * per_b + i, h)),
            pl.BlockSpec((None, seq, HEAD_DIM), lambda b, h, i: (3, b, h)),
            pl.BlockSpec((None, seq, HEAD_DIM), lambda b, h, i: (4, b, h)),
            pl.BlockSpec((None, tq, HEAD_DIM), lambda b, h, i: (5, b * per_b + i, h)),
            pl.BlockSpec((None, n_pat, 3, SUB, WIN), lambda b, h, i: (h, 0, 0, 0, 0)),
        ],
        out_specs=pl.BlockSpec((tq, HEAD_DIM), lambda b, h, i: (b * per_b + i, h)),
        out_shape=jax.ShapeDtypeStruct((bsz * seq, heads * HEAD_DIM), BF16),
        scratch_shapes=[
            pltpu.VMEM((n_pat, seq, HEAD_DIM), BF16),
            pltpu.VMEM((n_pat, seq, HEAD_DIM), BF16),
            pltpu.VMEM((n_pat, tq, HEAD_DIM), F32),
            pltpu.VMEM((n_pat, tq, HEAD_DIM), F32),
            pltpu.VMEM((n_pat, tq, HEAD_DIM), F32),
        ],
        compiler_params=pltpu.CompilerParams(
            dimension_semantics=("arbitrary", "arbitrary", "arbitrary"),
            vmem_limit_bytes=VMEM_LIMIT),
        name="dilated_attn",
    )(planes, planes, planes, planes, tbl)


def _out_proj_kernel(ya_ref, yb_ref, x_ref, mod_ref, w_ref, fg_ref, o_ref, *, final):
    half = ya_ref.shape[1]
    y = (jnp.dot(ya_ref[...], w_ref[0:half, :], preferred_element_type=F32)
         + jnp.dot(yb_ref[...], w_ref[half:, :], preferred_element_type=F32))
    xn = x_ref[...] + mod_ref[2:3, :] * y
    if final:
        xn = _rms(xn, fg_ref[...])
    o_ref[...] = xn


def _out_proj(ya, yb, x2, mod_l, w_l, fg, seq, final):
    rows, d = x2.shape
    tm = TM_OUT
    per_b = seq // tm
    half = ya.shape[1]
    kern = functools.partial(_out_proj_kernel, final=final)
    return pl.pallas_call(
        kern,
        grid=(rows // tm,),
        in_specs=[
            pl.BlockSpec((tm, half), lambda i: (i, 0)),
            pl.BlockSpec((tm, half), lambda i: (i, 0)),
            pl.BlockSpec((tm, d), lambda i: (i, 0)),
            pl.BlockSpec((None, 3, d), lambda i: (i // per_b, 0, 0)),
            pl.BlockSpec(w_l.shape, lambda i: (0, 0)),
            pl.BlockSpec((1, d), lambda i: (0, 0)),
        ],
        out_specs=pl.BlockSpec((tm, d), lambda i: (i, 0)),
        out_shape=jax.ShapeDtypeStruct((rows, d), F32),
        input_output_aliases={2: 0},
        compiler_params=pltpu.CompilerParams(
            dimension_semantics=("arbitrary",), vmem_limit_bytes=VMEM_LIMIT),
        name="out_proj",
    )(ya, yb, x2, mod_l, w_l, fg)


def _t5_buckets(rel):
    nb = NUM_BUCKETS // 2
    max_exact = nb // 2
    base = np.where(rel > 0, nb, 0)
    n = np.abs(rel)
    large = max_exact + (np.log(np.maximum(n, 1) / max_exact)
                         / math.log(MAX_DISTANCE / max_exact) * (nb - max_exact)).astype(np.int32)
    large = np.minimum(large, nb - 1)
    return (base + np.where(n < max_exact, n, large)).astype(np.int32)


def _bias_tables(rel_bias):
    qi = np.arange(SUB)[:, None]
    kj = np.arange(WIN)[None, :]
    tabs = []
    for _, d in PATTERNS:
        per_var = []
        for var in range(3):
            j = kj - qi - var * HALF_WIN
            band = np.abs(j) <= HALF_WIN
            buckets = _t5_buckets(np.where(band, j, 0) * d)
            vals = jnp.take(rel_bias, jnp.asarray(buckets.reshape(-1)), axis=0)
            vals = vals.reshape(SUB, WIN, -1)
            per_var.append(jnp.where(jnp.asarray(band)[:, :, None], vals, NEG))
        tabs.append(jnp.stack(per_var))
    return jnp.stack(tabs).transpose(4, 0, 1, 2, 3).astype(F32)


def _rope_tables(positions):
    inv = 1.0 / (ROPE_THETA ** (jnp.arange(0, ROPE_DIM, 2, dtype=F32) / ROPE_DIM))
    ang = positions.astype(F32)[..., None] * inv
    cos, sin = jnp.cos(ang), jnp.sin(ang)
    ctab = jnp.concatenate([cos, cos, -sin, sin], axis=-1)
    return cos.swapaxes(1, 2), sin.swapaxes(1, 2), ctab


def kernel(x, c, positions, norm_g, ada_w, ada_b, w_in, q_a_norm_g, w_q_up, kv_a_norm_g,
           w_kv_up, rel_bias, w_out, final_norm_g):
    bsz, seq, d = x.shape
    depth = w_in.shape[0]
    heads = w_q_up.shape[2] // (HEAD_DIM + ROPE_DIM)
    width = heads * HEAD_DIM

    cq_w = w_in[:, :, 0:Q_LORA]
    ckv_w = w_in[:, :, Q_LORA:Q_LORA + KV_LORA]
    kr_w = w_in[:, :, Q_LORA + KV_LORA:Q_LORA + KV_LORA + ROPE_DIM]
    k1, k2 = kr_w[..., :HALF_ROPE], kr_w[..., HALF_ROPE:]
    rest = w_in[:, :, Q_LORA + KV_LORA + ROPE_DIM:]
    pad = jnp.zeros((depth, d, GROUP - Q_LORA - KV_LORA - 2 * ROPE_DIM), w_in.dtype)
    w_in_g = jnp.concatenate([cq_w, ckv_w, k1, k2, k2, k1, pad, rest], axis=-1).astype(BF16)

    wq = w_q_up.reshape(depth, Q_LORA, heads, HEAD_DIM + ROPE_DIM)
    wq = jnp.pad(wq, ((0, 0), (0, 0), (0, 0), (0, QK_PAD - HEAD_DIM - ROPE_DIM)))
    wqt = wq.reshape(depth, Q_LORA, heads * QK_PAD).swapaxes(1, 2).astype(BF16)
    wkv = w_kv_up.reshape(depth, KV_LORA, heads, 2 * HEAD_DIM)
    wk = wkv[..., :HEAD_DIM].reshape(depth, KV_LORA, width).astype(BF16)
    wvt = wkv[..., HEAD_DIM:].reshape(depth, KV_LORA, width).swapaxes(1, 2).astype(BF16)
    w_out_b = w_out.astype(BF16)

    cos_t, sin_t, ctab = _rope_tables(positions)
    tbl = _bias_tables(rel_bias)

    c8 = jnp.pad(c, ((0, 8 - bsz), (0, 0)))
    mod = _adaln(c8, ada_w, ada_b)[:, :bsz].reshape(depth, bsz, 3, d)

    x2 = x.reshape(bsz * seq, d)
    fg = final_norm_g.reshape(1, d)
    for l in range(depth):
        planes = _in_proj(x2, mod[l], norm_g[l].reshape(1, d), w_in_g[l], seq)
        qt, k, vt = _mla_prep(planes, cos_t, sin_t, ctab, q_a_norm_g[l].reshape(1, Q_LORA),
                              kv_a_norm_g[l].reshape(1, KV_LORA), wqt[l], wk[l], wvt[l],
                              bsz, seq, heads)
        ya = _mla_attn(qt, k, vt, planes, bsz, seq, heads)
        yb = _dilated_attn(planes, tbl, bsz, seq, heads)
        x2 = _out_proj(ya, yb, x2, mod[l], w_out_b[l], fg, seq, final=(l == depth - 1))
    return x2.reshape(bsz, seq, d)
```

```python
import functools
import math

import numpy as np
import jax
import jax.numpy as jnp
from jax import lax
from jax.experimental import pallas as pl
from jax.experimental.pallas import tpu as pltpu

F32 = jnp.float32
BF16 = jnp.bfloat16

HEAD_DIM = 128
ROPE_DIM = 64
HALF_ROPE = ROPE_DIM // 2
QK_PAD = 256
V_ROWS = HEAD_DIM + 16
Q_LORA = 512
KV_LORA = 256
GROUP = 1024
N_GROUPS = 6
PATTERNS = ((128, 1), (512, 4), (2048, 16))
HALF_WIN = 64
NUM_BUCKETS = 32
MAX_DISTANCE = 1024
ROPE_THETA = 10000.0
EPS = 1e-6
NEG = -1e30
LOG2E = math.log2(math.e)
MLA_EXP_SCALE = LOG2E / math.sqrt(HEAD_DIM + ROPE_DIM)

VMEM_LIMIT = 56 * 1024 * 1024

TM_IN = 1024
NORM_CHUNKS = 4
TS_PREP = 512
TQ_MLA = 1024
TK_MLA = 512
STEPS_MLA = 8
TQ_DIL = 2048
UNROLL_DIL = 16
TM_OUT = 512
SUB = 128
WIN = 256


def _nt_dot(a, b):
    return lax.dot_general(a, b, (((1,), (1,)), ((), ())), preferred_element_type=F32)


def _rms(x, g):
    return x * lax.rsqrt(jnp.mean(x * x, axis=-1, keepdims=True) + EPS) * g


def _silu(x):
    return x * jax.nn.sigmoid(x)


def _adaln_kernel(c_ref, w_ref, b_ref, o_ref):
    ca = _silu(c_ref[...]).astype(BF16)
    o_ref[...] = jnp.dot(ca, w_ref[...].astype(BF16), preferred_element_type=F32) + b_ref[...]


def _adaln(c8, ada_w, ada_b):
    depth, d, n3 = ada_w.shape
    tn = 768
    return pl.pallas_call(
        _adaln_kernel,
        grid=(depth, n3 // tn),
        in_specs=[
            pl.BlockSpec((8, d), lambda l, j: (0, 0)),
            pl.BlockSpec((None, d, tn), lambda l, j: (l, 0, j)),
            pl.BlockSpec((None, 1, tn), lambda l, j: (l, 0, j)),
        ],
        out_specs=pl.BlockSpec((None, 8, tn), lambda l, j: (l, 0, j)),
        out_shape=jax.ShapeDtypeStruct((depth, 8, n3), F32),
        compiler_params=pltpu.CompilerParams(
            dimension_semantics=("arbitrary", "arbitrary"), vmem_limit_bytes=VMEM_LIMIT),
        name="adaln",
    )(c8, ada_w, ada_b.reshape(depth, 1, n3))


def _in_proj_kernel(x_ref, mod_ref, g_ref, w_ref, o_ref, h_sc):
    first = pl.program_id(1) == 0

    @pl.when(first)
    def _():
        chunk = x_ref.shape[0] // NORM_CHUNKS
        for c in range(NORM_CHUNKS):
            rows = slice(c * chunk, (c + 1) * chunk)
            y = _rms(x_ref[rows, :], g_ref[...])
            h = (y * (1.0 + mod_ref[1:2, :]) + mod_ref[0:1, :]).astype(BF16)
            h_sc[rows, :] = h
            o_ref[rows, :] = jnp.dot(h, w_ref[...], preferred_element_type=F32)

    @pl.when(jnp.logical_not(first))
    def _():
        o_ref[...] = jnp.dot(h_sc[...], w_ref[...], preferred_element_type=F32)


def _in_proj(x2, mod_l, g_l, w_l, seq):
    rows, d = x2.shape
    tm = TM_IN
    per_b = seq // tm
    return pl.pallas_call(
        _in_proj_kernel,
        grid=(rows // tm, N_GROUPS),
        in_specs=[
            pl.BlockSpec((tm, d), lambda i, j: (i, 0)),
            pl.BlockSpec((None, 3, d), lambda i, j: (i // per_b, 0, 0)),
            pl.BlockSpec((1, d), lambda i, j: (0, 0)),
            pl.BlockSpec((d, GROUP), lambda i, j: (0, j)),
        ],
        out_specs=pl.BlockSpec((None, tm, GROUP), lambda i, j: (j, i, 0)),
        out_shape=jax.ShapeDtypeStruct((N_GROUPS, rows, GROUP), F32),
        scratch_shapes=[pltpu.VMEM((tm, d), BF16)],
        compiler_params=pltpu.CompilerParams(
            dimension_semantics=("arbitrary", "arbitrary"), vmem_limit_bytes=VMEM_LIMIT),
        name="in_proj",
    )(x2, mod_l, g_l, w_l)


def _mla_prep_kernel(lat_ref, cos_ref, sin_ref, ctab_ref, qg_ref, kvg_ref,
                     wqt_ref, wk_ref, wvt_ref, qt_ref, k_ref, vt_ref, *, heads):
    ts = lat_ref.shape[0]
    cqn = _rms(lat_ref[:, 0:Q_LORA], qg_ref[...]).astype(BF16)
    ckvn = _rms(lat_ref[:, Q_LORA:Q_LORA + KV_LORA], kvg_ref[...]).astype(BF16)

    qt = _nt_dot(wqt_ref[...], cqn)
    cos_t = cos_ref[...]
    sin_t = sin_ref[...]
    for h in range(heads):
        base = h * QK_PAD
        qt_ref[h, 0:HEAD_DIM, :] = (qt[base:base + HEAD_DIM] * MLA_EXP_SCALE).astype(BF16)
        t1 = qt[base + HEAD_DIM:base + HEAD_DIM + HALF_ROPE]
        t2 = qt[base + HEAD_DIM + HALF_ROPE:base + HEAD_DIM + ROPE_DIM]
        qt_ref[h, HEAD_DIM:HEAD_DIM + HALF_ROPE, :] = (
            (t1 * cos_t - t2 * sin_t) * MLA_EXP_SCALE).astype(BF16)
        qt_ref[h, HEAD_DIM + HALF_ROPE:HEAD_DIM + ROPE_DIM, :] = (
            (t2 * cos_t + t1 * sin_t) * MLA_EXP_SCALE).astype(BF16)
        qt_ref[h, HEAD_DIM + ROPE_DIM:QK_PAD, :] = jnp.zeros((QK_PAD - HEAD_DIM - ROPE_DIM, ts), BF16)

    y = lat_ref[:, Q_LORA + KV_LORA:Q_LORA + KV_LORA + 128] * ctab_ref[...]
    folded = y + pltpu.roll(y, 64, axis=1)
    lane = lax.broadcasted_iota(jnp.int32, folded.shape, 1)
    k_rope = jnp.where(lane < ROPE_DIM, folded, 0.0).astype(BF16)

    kn = jnp.dot(ckvn, wk_ref[...], preferred_element_type=F32)
    vt = _nt_dot(wvt_ref[...], ckvn)
    for h in range(heads):
        k_ref[h, :, 0:HEAD_DIM] = kn[:, h * HEAD_DIM:(h + 1) * HEAD_DIM].astype(BF16)
        k_ref[h, :, HEAD_DIM:QK_PAD] = k_rope
        for j in range(ts // TK_MLA):
            vt_ref[h, j, 0:HEAD_DIM, :] = vt[h * HEAD_DIM:(h + 1) * HEAD_DIM,
                                             j * TK_MLA:(j + 1) * TK_MLA].astype(BF16)
            vt_ref[h, j, HEAD_DIM:V_ROWS, :] = jnp.ones((V_ROWS - HEAD_DIM, TK_MLA), BF16)


def _mla_prep(planes, cos_t, sin_t, ctab, qg, kvg, wqt, wk, wvt, bsz, seq, heads):
    ts = TS_PREP
    per_b = seq // ts
    kern = functools.partial(_mla_prep_kernel, heads=heads)
    return pl.pallas_call(
        kern,
        grid=(bsz, per_b),
        in_specs=[
            pl.BlockSpec((None, ts, GROUP), lambda b, i: (0, b * per_b + i, 0)),
            pl.BlockSpec((None, HALF_ROPE, ts), lambda b, i: (b, 0, i)),
            pl.BlockSpec((None, HALF_ROPE, ts), lambda b, i: (b, 0, i)),
            pl.BlockSpec((None, ts, 128), lambda b, i: (b, i, 0)),
            pl.BlockSpec((1, Q_LORA), lambda b, i: (0, 0)),
            pl.BlockSpec((1, KV_LORA), lambda b, i: (0, 0)),
            pl.BlockSpec(wqt.shape, lambda b, i: (0, 0)),
            pl.BlockSpec(wk.shape, lambda b, i: (0, 0)),
            pl.BlockSpec(wvt.shape, lambda b, i: (0, 0)),
        ],
        out_specs=[
            pl.BlockSpec((None, heads, QK_PAD, ts), lambda b, i: (b, 0, 0, i)),
            pl.BlockSpec((None, heads, ts, QK_PAD), lambda b, i: (b, 0, i, 0)),
            pl.BlockSpec((None, heads, ts // TK_MLA, V_ROWS, TK_MLA), lambda b, i: (b, 0, i, 0, 0)),
        ],
        out_shape=[
            jax.ShapeDtypeStruct((bsz, heads, QK_PAD, seq), BF16),
            jax.ShapeDtypeStruct((bsz, heads, seq, QK_PAD), BF16),
            jax.ShapeDtypeStruct((bsz, heads, seq // TK_MLA, V_ROWS, TK_MLA), BF16),
        ],
        compiler_params=pltpu.CompilerParams(
            dimension_semantics=("arbitrary", "arbitrary"), vmem_limit_bytes=VMEM_LIMIT),
        name="mla_prep",
    )(planes, cos_t, sin_t, ctab, qg, kvg, wqt, wk, wvt)


def _mla_attn_kernel(qt_ref, k_ref, vt_ref, g_ref, o_ref, s_sc, m_sc, acc_sc):
    n_kt = vt_ref.shape[0]
    m_sc[...] = jnp.full(m_sc.shape, -jnp.inf, F32)
    acc_sc[...] = jnp.zeros(acc_sc.shape, F32)

    def scores(kt):
        k = k_ref[pl.ds(pl.multiple_of(kt * TK_MLA, TK_MLA), TK_MLA), :]
        return jnp.dot(k, qt_ref[...], preferred_element_type=F32)

    def step(kt, slot):
        s_sc[1 - slot] = scores(jnp.minimum(kt + 1, n_kt - 1))
        s = s_sc[slot]
        m_prev = m_sc[...]
        m_new = jnp.maximum(m_prev, jnp.max(s, axis=0, keepdims=True))
        alpha = jnp.exp2(m_prev - m_new)
        p = jnp.exp2(s - m_new)
        acc_sc[...] = alpha * acc_sc[...] + jnp.dot(vt_ref[kt], p.astype(BF16),
                                                    preferred_element_type=F32)
        m_sc[...] = m_new

    s_sc[0] = scores(0)

    def body(i, carry):
        for u in range(STEPS_MLA):
            step(STEPS_MLA * i + u, u % 2)
        return carry

    lax.fori_loop(0, n_kt // STEPS_MLA, body, 0)
    o_t = acc_sc[0:HEAD_DIM, :] / acc_sc[HEAD_DIM:HEAD_DIM + 1, :]
    o_ref[...] = (o_t.T * _silu(g_ref[...])).astype(BF16)


def _mla_attn(qt, k, vt, planes, bsz, seq, heads):
    tq = TQ_MLA
    per_b = seq // tq
    n_kt = seq // TK_MLA
    return pl.pallas_call(
        _mla_attn_kernel,
        grid=(bsz, heads, per_b),
        in_specs=[
            pl.BlockSpec((None, None, QK_PAD, tq), lambda b, h, i: (b, h, 0, i)),
            pl.BlockSpec((None, None, seq, QK_PAD), lambda b, h, i: (b, h, 0, 0)),
            pl.BlockSpec((None, None, n_kt, V_ROWS, TK_MLA), lambda b, h, i: (b, h, 0, 0, 0)),
            pl.BlockSpec((None, tq, HEAD_DIM), lambda b, h, i: (1, b * per_b + i, h)),
        ],
        out_specs=pl.BlockSpec((tq, HEAD_DIM), lambda b, h, i: (b * per_b + i, h)),
        out_shape=jax.ShapeDtypeStruct((bsz * seq, heads * HEAD_DIM), BF16),
        scratch_shapes=[pltpu.VMEM((2, TK_MLA, tq), F32), pltpu.VMEM((1, tq), F32),
                        pltpu.VMEM((V_ROWS, tq), F32)],
        compiler_params=pltpu.CompilerParams(
            dimension_semantics=("arbitrary", "arbitrary", "arbitrary"),
            vmem_limit_bytes=VMEM_LIMIT),
        name="mla_attn",
    )(qt, k, vt, planes)


def _dilated_kernel(q_ref, k_ref, v_ref, g_ref, tbl_ref, o_ref,
                    kc_sc, vc_sc, o_sc, m_sc, l_sc, *, seq, scale):
    tq = q_ref.shape[0]
    step = pl.program_id(2)

    @pl.when(step == 0)
    def _():
        chunk = 256
        for g, (_, d) in enumerate(PATTERNS):
            cls_len = seq // d
            for r in range(d):
                def copy(c, carry, g=g, d=d, r=r, cls_len=cls_len):
                    src = pl.ds(r + d * chunk * c, chunk, stride=d) if d > 1 else pl.ds(chunk * c, chunk)
                    dst = pl.ds(pl.multiple_of(r * cls_len + chunk * c, chunk), chunk)
                    kc_sc[g, dst, :] = k_ref[src, :].astype(BF16)
                    vc_sc[g, dst, :] = v_ref[src, :].astype(BF16)
                    return carry
                lax.fori_loop(0, cls_len // chunk, copy, 0)

    t0 = step * tq
    n_units = tq // SUB
    for g, (_, d) in enumerate(PATTERNS):
        cls_len = seq // d
        sb_bits = (n_units // d).bit_length() - 1

        def unit(u, carry, g=g, d=d, cls_len=cls_len, sb_bits=sb_bits):
            r = lax.shift_right_logical(u, sb_bits)
            sb = u & ((1 << sb_bits) - 1)
            row0 = r + d * SUB * sb
            rows = pl.ds(row0, SUB, stride=d) if d > 1 else pl.ds(pl.multiple_of(row0, SUB), SUB)
            qc = q_ref[rows, :].astype(BF16)
            i0 = t0 // d + sb * SUB
            ws = jnp.clip(i0 - HALF_WIN, 0, cls_len - WIN)
            var = (i0 - ws) // HALF_WIN
            win = pl.ds(pl.multiple_of(r * cls_len + ws, HALF_WIN), WIN)
            s = _nt_dot(qc, kc_sc[g, win, :]) * scale + tbl_ref[g, var]
            m = jnp.max(s, axis=-1, keepdims=True)
            p = jnp.exp2(s - m)
            o_sc[g, rows, :] = jnp.dot(p.astype(BF16), vc_sc[g, win, :], preferred_element_type=F32)
            m_sc[g, rows, :] = jnp.broadcast_to(m, (SUB, HEAD_DIM))
            l_sc[g, rows, :] = jnp.broadcast_to(jnp.sum(p, axis=-1, keepdims=True), (SUB, HEAD_DIM))
            return carry

        lax.fori_loop(0, n_units, unit, 0, unroll=UNROLL_DIL)

    m_all = jnp.maximum(jnp.maximum(m_sc[0], m_sc[1]), m_sc[2])
    num = jnp.zeros((tq, HEAD_DIM), F32)
    tot = jnp.zeros((tq, HEAD_DIM), F32)
    for g in range(len(PATTERNS)):
        w = jnp.exp2(m_sc[g] - m_all)
        num = num + w * o_sc[g]
        tot = tot + w * l_sc[g]
    o_ref[...] = ((num / tot) * _silu(g_ref[...])).astype(BF16)


def _dilated_attn(planes, tbl, bsz, seq, heads):
    tq = TQ_DIL
    per_b = seq // tq
    n_pat = len(PATTERNS)
    kern = functools.partial(_dilated_kernel, seq=seq, scale=LOG2E * HEAD_DIM ** -0.5)
    return pl.pallas_call(
        kern,
        grid=(bsz, heads, per_b),
        in_specs=[
            pl.BlockSpec((None, tq, HEAD_DIM), lambda b, h, i: (2, b * per_b + i, h)),
            pl.BlockSpec((None, seq, HEAD_DIM), lambda b, h, i: (3, b, h)),
            pl.BlockSpec((None, seq, HEAD_DIM), lambda b, h, i: (4, b, h)),
            pl.BlockSpec((None, tq, HEAD_DIM), lambda b, h, i: (5, b * per_b + i, h)),
            pl.BlockSpec((None, n_pat, 3, SUB, WIN), lambda b, h, i: (h, 0, 0, 0, 0)),
        ],
        out_specs=pl.BlockSpec((tq, HEAD_DIM), lambda b, h, i: (b * per_b + i, h)),
        out_shape=jax.ShapeDtypeStruct((bsz * seq, heads * HEAD_DIM), BF16),
        scratch_shapes=[
            pltpu.VMEM((n_pat, seq, HEAD_DIM), BF16),
            pltpu.VMEM((n_pat, seq, HEAD_DIM), BF16),
            pltpu.VMEM((n_pat, tq, HEAD_DIM), F32),
            pltpu.VMEM((n_pat, tq, HEAD_DIM), F32),
            pltpu.VMEM((n_pat, tq, HEAD_DIM), F32),
        ],
        compiler_params=pltpu.CompilerParams(
            dimension_semantics=("arbitrary", "arbitrary", "arbitrary"),
            vmem_limit_bytes=VMEM_LIMIT),
        name="dilated_attn",
    )(planes, planes, planes, planes, tbl)


def _out_proj_kernel(ya_ref, yb_ref, x_ref, mod_ref, w_ref, fg_ref, o_ref, *, final):
    half = ya_ref.shape[1]
    y = (jnp.dot(ya_ref[...], w_ref[0:half, :], preferred_element_type=F32)
         + jnp.dot(yb_ref[...], w_ref[half:, :], preferred_element_type=F32))
    xn = x_ref[...] + mod_ref[2:3, :] * y
    if final:
        xn = _rms(xn, fg_ref[...])
    o_ref[...] = xn


def _out_proj(ya, yb, x2, mod_l, w_l, fg, seq, final, in_place):
    rows, d = x2.shape
    tm = TM_OUT
    per_b = seq // tm
    half = ya.shape[1]
    kern = functools.partial(_out_proj_kernel, final=final)
    return pl.pallas_call(
        kern,
        grid=(rows // tm,),
        in_specs=[
            pl.BlockSpec((tm, half), lambda i: (i, 0)),
            pl.BlockSpec((tm, half), lambda i: (i, 0)),
            pl.BlockSpec((tm, d), lambda i: (i, 0)),
            pl.BlockSpec((None, 3, d), lambda i: (i // per_b, 0, 0)),
            pl.BlockSpec(w_l.shape, lambda i: (0, 0)),
            pl.BlockSpec((1, d), lambda i: (0, 0)),
        ],
        out_specs=pl.BlockSpec((tm, d), lambda i: (i, 0)),
        out_shape=jax.ShapeDtypeStruct((rows, d), F32),
        input_output_aliases={2: 0} if in_place else {},
        compiler_params=pltpu.CompilerParams(
            dimension_semantics=("arbitrary",), vmem_limit_bytes=VMEM_LIMIT),
        name="out_proj",
    )(ya, yb, x2, mod_l, w_l, fg)


def _t5_buckets(rel):
    nb = NUM_BUCKETS // 2
    max_exact = nb // 2
    base = np.where(rel > 0, nb, 0)
    n = np.abs(rel)
    large = max_exact + (np.log(np.maximum(n, 1) / max_exact)
                         / math.log(MAX_DISTANCE / max_exact) * (nb - max_exact)).astype(np.int32)
    large = np.minimum(large, nb - 1)
    return (base + np.where(n < max_exact, n, large)).astype(np.int32)


def _bias_tables(rel_bias):
    heads = rel_bias.shape[1]
    n = WIN + SUB
    pos = np.arange(n)
    delta = np.where(pos < WIN, pos, pos - n)
    rows = []
    for _, d in PATTERNS:
        for var in range(3):
            j = delta - var * HALF_WIN
            band = np.abs(j) <= HALF_WIN
            rows.append(np.where(band, _t5_buckets(np.where(band, j, 0) * d), NUM_BUCKETS))
    idx = np.stack(rows).reshape(-1)
    ext = jnp.concatenate([rel_bias.astype(F32) * LOG2E, jnp.full((1, heads), NEG, F32)], axis=0)
    gen = jnp.take(ext, jnp.asarray(idx), axis=0).reshape(len(rows), n, heads).transpose(2, 0, 1)
    tab = jnp.tile(gen, (1, 1, SUB))[:, :, :SUB * (n - 1)].reshape(heads, len(rows), SUB, n - 1)
    return tab[..., :WIN].reshape(heads, len(PATTERNS), 3, SUB, WIN)


def _rope_tables(positions):
    inv = 1.0 / (ROPE_THETA ** (jnp.arange(0, ROPE_DIM, 2, dtype=F32) / ROPE_DIM))
    ang = positions.astype(F32)[..., None] * inv
    cos, sin = jnp.cos(ang), jnp.sin(ang)
    ctab = jnp.concatenate([cos, cos, -sin, sin], axis=-1)
    return cos.swapaxes(1, 2), sin.swapaxes(1, 2), ctab


def kernel(x, c, positions, norm_g, ada_w, ada_b, w_in, q_a_norm_g, w_q_up, kv_a_norm_g,
           w_kv_up, rel_bias, w_out, final_norm_g):
    bsz, seq, d = x.shape
    depth = w_in.shape[0]
    heads = w_q_up.shape[2] // (HEAD_DIM + ROPE_DIM)
    width = heads * HEAD_DIM

    cq_w = w_in[:, :, 0:Q_LORA]
    ckv_w = w_in[:, :, Q_LORA:Q_LORA + KV_LORA]
    kr_w = w_in[:, :, Q_LORA + KV_LORA:Q_LORA + KV_LORA + ROPE_DIM]
    k1, k2 = kr_w[..., :HALF_ROPE], kr_w[..., HALF_ROPE:]
    rest = w_in[:, :, Q_LORA + KV_LORA + ROPE_DIM:]
    pad = jnp.zeros((depth, d, GROUP - Q_LORA - KV_LORA - 2 * ROPE_DIM), w_in.dtype)
    w_in_g = jnp.concatenate([cq_w, ckv_w, k1, k2, k2, k1, pad, rest], axis=-1).astype(BF16)

    wq = w_q_up.reshape(depth, Q_LORA, heads, HEAD_DIM + ROPE_DIM)
    wq = jnp.pad(wq, ((0, 0), (0, 0), (0, 0), (0, QK_PAD - HEAD_DIM - ROPE_DIM)))
    wqt = wq.reshape(depth, Q_LORA, heads * QK_PAD).swapaxes(1, 2).astype(BF16)
    wkv = w_kv_up.reshape(depth, KV_LORA, heads, 2 * HEAD_DIM)
    wk = wkv[..., :HEAD_DIM].reshape(depth, KV_LORA, width).astype(BF16)
    wvt = wkv[..., HEAD_DIM:].reshape(depth, KV_LORA, width).swapaxes(1, 2).astype(BF16)
    w_out_b = w_out.astype(BF16)

    cos_t, sin_t, ctab = _rope_tables(positions)
    tbl = _bias_tables(rel_bias)

    c8 = jnp.pad(c, ((0, 8 - bsz), (0, 0)))
    mod = _adaln(c8, ada_w, ada_b)[:, :bsz].reshape(depth, bsz, 3, d)

    x2 = x.reshape(bsz * seq, d)
    fg = final_norm_g.reshape(1, d)
    for l in range(depth):
        planes = _in_proj(x2, mod[l], norm_g[l].reshape(1, d), w_in_g[l], seq)
        qt, k, vt = _mla_prep(planes, cos_t, sin_t, ctab, q_a_norm_g[l].reshape(1, Q_LORA),
                              kv_a_norm_g[l].reshape(1, KV_LORA), wqt[l], wk[l], wvt[l],
                              bsz, seq, heads)
        ya = _mla_attn(qt, k, vt, planes, bsz, seq, heads)
        yb = _dilated_attn(planes, tbl, bsz, seq, heads)
        x2 = _out_proj(ya, yb, x2, mod[l], w_out_b[l], fg, seq, final=(l == depth - 1),
                       in_place=(l > 0))
    return x2.reshape(bsz, seq, d)
```

```python
import functools
import math

import numpy as np
import jax
import jax.numpy as jnp
from jax import lax
from jax.experimental import pallas as pl
from jax.experimental.pallas import tpu as pltpu

F32 = jnp.float32
BF16 = jnp.bfloat16

HEAD_DIM = 128
ROPE_DIM = 64
HALF_ROPE = ROPE_DIM // 2
QK_PAD = 256
V_ROWS = HEAD_DIM + 16
Q_LORA = 512
KV_LORA = 256
GROUP = 1024
N_GROUPS = 6
PATTERNS = ((128, 1), (512, 4), (2048, 16))
HALF_WIN = 64
NUM_BUCKETS = 32
MAX_DISTANCE = 1024
ROPE_THETA = 10000.0
EPS = 1e-6
NEG = -1e30
LOG2E = math.log2(math.e)
MLA_EXP_SCALE = LOG2E / math.sqrt(HEAD_DIM + ROPE_DIM)

VMEM_LIMIT = 56 * 1024 * 1024

TM_IN = 1024
NORM_CHUNKS = 4
TS_PREP = 512
TQ_MLA = 1024
TK_MLA = 512
STEPS_MLA = 8
TQ_DIL = 2048
UNROLL_DIL = 16
TM_OUT = 512
SUB = 128
WIN = 256


def _nt_dot(a, b):
    return lax.dot_general(a, b, (((1,), (1,)), ((), ())), preferred_element_type=F32)


def _rms(x, g):
    return x * lax.rsqrt(jnp.mean(x * x, axis=-1, keepdims=True) + EPS) * g


def _silu(x):
    return x * jax.nn.sigmoid(x)


def _adaln_kernel(c_ref, w_ref, b_ref, o_ref):
    ca = _silu(c_ref[...]).astype(BF16)
    o_ref[...] = jnp.dot(ca, w_ref[...].astype(BF16), preferred_element_type=F32) + b_ref[...]


def _adaln(c8, ada_w, ada_b):
    depth, d, n3 = ada_w.shape
    tn = 768
    return pl.pallas_call(
        _adaln_kernel,
        grid=(depth, n3 // tn),
        in_specs=[
            pl.BlockSpec((8, d), lambda l, j: (0, 0)),
            pl.BlockSpec((None, d, tn), lambda l, j: (l, 0, j)),
            pl.BlockSpec((None, 1, tn), lambda l, j: (l, 0, j)),
        ],
        out_specs=pl.BlockSpec((None, 8, tn), lambda l, j: (l, 0, j)),
        out_shape=jax.ShapeDtypeStruct((depth, 8, n3), F32),
        compiler_params=pltpu.CompilerParams(
            dimension_semantics=("arbitrary", "arbitrary"), vmem_limit_bytes=VMEM_LIMIT),
        name="adaln",
    )(c8, ada_w, ada_b.reshape(depth, 1, n3))


def _in_proj_kernel(x_ref, mod_ref, g_ref, wlat_ref, w_ref, o_ref, h_sc):
    first = pl.program_id(1) == 0

    @pl.when(first)
    def _():
        chunk = x_ref.shape[0] // NORM_CHUNKS
        for c in range(NORM_CHUNKS):
            rows = slice(c * chunk, (c + 1) * chunk)
            y = _rms(x_ref[rows, :], g_ref[...])
            h = (y * (1.0 + mod_ref[1:2, :]) + mod_ref[0:1, :]).astype(BF16)
            h_sc[rows, :] = h
            o_ref[rows, :] = jnp.dot(h, wlat_ref[...], preferred_element_type=F32)

    @pl.when(jnp.logical_not(first))
    def _():
        o_ref[...] = jnp.dot(h_sc[...], w_ref[...], preferred_element_type=F32)


def _in_proj(x2, mod, norm_g, w_lat, w_rest, layer, seq):
    rows, d = x2.shape
    tm = TM_IN
    per_b = seq // tm
    return pl.pallas_call(
        _in_proj_kernel,
        grid=(rows // tm, N_GROUPS),
        in_specs=[
            pl.BlockSpec((tm, d), lambda i, j: (i, 0)),
            pl.BlockSpec((None, None, 3, d), lambda i, j: (layer, i // per_b, 0, 0)),
            pl.BlockSpec((None, 1, d), lambda i, j: (layer, 0, 0)),
            pl.BlockSpec((None, d, GROUP), lambda i, j: (layer, 0, 0)),
            pl.BlockSpec((None, d, GROUP), lambda i, j: (layer, 0, jnp.maximum(j - 1, 0))),
        ],
        out_specs=pl.BlockSpec((None, tm, GROUP), lambda i, j: (j, i, 0)),
        out_shape=jax.ShapeDtypeStruct((N_GROUPS, rows, GROUP), F32),
        scratch_shapes=[pltpu.VMEM((tm, d), BF16)],
        compiler_params=pltpu.CompilerParams(
            dimension_semantics=("arbitrary", "arbitrary"), vmem_limit_bytes=VMEM_LIMIT),
        name="in_proj",
    )(x2, mod, norm_g, w_lat, w_rest)


def _mla_prep_kernel(lat_ref, cos_ref, sin_ref, ctab_ref, qg_ref, kvg_ref,
                     wqt_ref, wk_ref, wvt_ref, qt_ref, k_ref, vt_ref, *, heads):
    ts = lat_ref.shape[0]
    cqn = _rms(lat_ref[:, 0:Q_LORA], qg_ref[...]).astype(BF16)
    ckvn = _rms(lat_ref[:, Q_LORA:Q_LORA + KV_LORA], kvg_ref[...]).astype(BF16)

    qt = _nt_dot(wqt_ref[...], cqn)
    cos_t = cos_ref[...]
    sin_t = sin_ref[...]
    for h in range(heads):
        base = h * QK_PAD
        qt_ref[h, 0:HEAD_DIM, :] = (qt[base:base + HEAD_DIM] * MLA_EXP_SCALE).astype(BF16)
        t1 = qt[base + HEAD_DIM:base + HEAD_DIM + HALF_ROPE]
        t2 = qt[base + HEAD_DIM + HALF_ROPE:base + HEAD_DIM + ROPE_DIM]
        qt_ref[h, HEAD_DIM:HEAD_DIM + HALF_ROPE, :] = (
            (t1 * cos_t - t2 * sin_t) * MLA_EXP_SCALE).astype(BF16)
        qt_ref[h, HEAD_DIM + HALF_ROPE:HEAD_DIM + ROPE_DIM, :] = (
            (t2 * cos_t + t1 * sin_t) * MLA_EXP_SCALE).astype(BF16)
        qt_ref[h, HEAD_DIM + ROPE_DIM:QK_PAD, :] = jnp.zeros((QK_PAD - HEAD_DIM - ROPE_DIM, ts), BF16)

    y = lat_ref[:, Q_LORA + KV_LORA:Q_LORA + KV_LORA + 128] * ctab_ref[...]
    folded = y + pltpu.roll(y, 64, axis=1)
    lane = lax.broadcasted_iota(jnp.int32, folded.shape, 1)
    k_rope = jnp.where(lane < ROPE_DIM, folded, 0.0).astype(BF16)

    kn = jnp.dot(ckvn, wk_ref[...], preferred_element_type=F32)
    vt = _nt_dot(wvt_ref[...], ckvn)
    for h in range(heads):
        k_ref[h, :, 0:HEAD_DIM] = kn[:, h * HEAD_DIM:(h + 1) * HEAD_DIM].astype(BF16)
        k_ref[h, :, HEAD_DIM:QK_PAD] = k_rope
        for j in range(ts // TK_MLA):
            vt_ref[h, j, 0:HEAD_DIM, :] = vt[h * HEAD_DIM:(h + 1) * HEAD_DIM,
                                             j * TK_MLA:(j + 1) * TK_MLA].astype(BF16)
            vt_ref[h, j, HEAD_DIM:V_ROWS, :] = jnp.ones((V_ROWS - HEAD_DIM, TK_MLA), BF16)


def _mla_prep(planes, cos_t, sin_t, ctab, qg, kvg, wqt, wk, wvt, layer, bsz, seq, heads):
    ts = TS_PREP
    per_b = seq // ts
    kern = functools.partial(_mla_prep_kernel, heads=heads)
    return pl.pallas_call(
        kern,
        grid=(bsz, per_b),
        in_specs=[
            pl.BlockSpec((None, ts, GROUP), lambda b, i: (0, b * per_b + i, 0)),
            pl.BlockSpec((None, HALF_ROPE, ts), lambda b, i: (b, 0, i)),
            pl.BlockSpec((None, HALF_ROPE, ts), lambda b, i: (b, 0, i)),
            pl.BlockSpec((None, ts, 128), lambda b, i: (b, i, 0)),
            pl.BlockSpec((None, 1, Q_LORA), lambda b, i: (layer, 0, 0)),
            pl.BlockSpec((None, 1, KV_LORA), lambda b, i: (layer, 0, 0)),
            pl.BlockSpec((None,) + wqt.shape[1:], lambda b, i: (layer, 0, 0)),
            pl.BlockSpec((None,) + wk.shape[1:], lambda b, i: (layer, 0, 0)),
            pl.BlockSpec((None,) + wvt.shape[1:], lambda b, i: (layer, 0, 0)),
        ],
        out_specs=[
            pl.BlockSpec((None, heads, QK_PAD, ts), lambda b, i: (b, 0, 0, i)),
            pl.BlockSpec((None, heads, ts, QK_PAD), lambda b, i: (b, 0, i, 0)),
            pl.BlockSpec((None, heads, ts // TK_MLA, V_ROWS, TK_MLA), lambda b, i: (b, 0, i, 0, 0)),
        ],
        out_shape=[
            jax.ShapeDtypeStruct((bsz, heads, QK_PAD, seq), BF16),
            jax.ShapeDtypeStruct((bsz, heads, seq, QK_PAD), BF16),
            jax.ShapeDtypeStruct((bsz, heads, seq // TK_MLA, V_ROWS, TK_MLA), BF16),
        ],
        compiler_params=pltpu.CompilerParams(
            dimension_semantics=("arbitrary", "arbitrary"), vmem_limit_bytes=VMEM_LIMIT),
        name="mla_prep",
    )(planes, cos_t, sin_t, ctab, qg, kvg, wqt, wk, wvt)


def _mla_attn_kernel(qt_ref, qtn_ref, k_ref, vt_ref, g_ref, o_ref, q_sc, s_sc, m_sc, acc_sc):
    n_kt = vt_ref.shape[0]
    m_sc[...] = jnp.full(m_sc.shape, -jnp.inf, F32)
    acc_sc[...] = jnp.zeros(acc_sc.shape, F32)
    q_sc[0] = qt_ref[...]
    q_sc[1] = qtn_ref[...]

    def scores(kt, which):
        k = k_ref[pl.ds(pl.multiple_of(kt * TK_MLA, TK_MLA), TK_MLA), :]
        return jnp.dot(k, q_sc[which], preferred_element_type=F32)

    @pl.when(pl.program_id(2) == 0)
    def _():
        s_sc[0] = scores(0, 0)

    def step(kt, slot):
        wrap = jnp.asarray(kt + 1 == n_kt).astype(jnp.int32)
        s_sc[1 - slot] = scores((kt + 1) * (1 - wrap), wrap)
        s = s_sc[slot]
        m_prev = m_sc[...]
        m_new = jnp.maximum(m_prev, jnp.max(s, axis=0, keepdims=True))
        alpha = jnp.exp2(m_prev - m_new)
        p = jnp.exp2(s - m_new)
        acc_sc[...] = alpha * acc_sc[...] + jnp.dot(vt_ref[kt], p.astype(BF16),
                                                    preferred_element_type=F32)
        m_sc[...] = m_new

    def body(i, carry):
        for u in range(STEPS_MLA):
            step(STEPS_MLA * i + u, u % 2)
        return carry

    lax.fori_loop(0, n_kt // STEPS_MLA, body, 0)
    o_t = acc_sc[0:HEAD_DIM, :] / acc_sc[HEAD_DIM:HEAD_DIM + 1, :]
    o_ref[...] = (o_t.T * _silu(g_ref[...])).astype(BF16)


def _mla_attn(qt, k, vt, planes, bsz, seq, heads):
    tq = TQ_MLA
    per_b = seq // tq
    n_kt = seq // TK_MLA
    return pl.pallas_call(
        _mla_attn_kernel,
        grid=(bsz, heads, per_b),
        in_specs=[
            pl.BlockSpec((None, None, QK_PAD, tq), lambda b, h, i: (b, h, 0, i)),
            pl.BlockSpec((None, None, QK_PAD, tq),
                         lambda b, h, i: (b, h, 0, jnp.minimum(i + 1, per_b - 1))),
            pl.BlockSpec((None, None, seq, QK_PAD), lambda b, h, i: (b, h, 0, 0)),
            pl.BlockSpec((None, None, n_kt, V_ROWS, TK_MLA), lambda b, h, i: (b, h, 0, 0, 0)),
            pl.BlockSpec((None, tq, HEAD_DIM), lambda b, h, i: (1, b * per_b + i, h)),
        ],
        out_specs=pl.BlockSpec((tq, HEAD_DIM), lambda b, h, i: (b * per_b + i, h)),
        out_shape=jax.ShapeDtypeStruct((bsz * seq, heads * HEAD_DIM), BF16),
        scratch_shapes=[pltpu.VMEM((2, QK_PAD, tq), BF16), pltpu.VMEM((2, TK_MLA, tq), F32),
                        pltpu.VMEM((1, tq), F32), pltpu.VMEM((V_ROWS, tq), F32)],
        compiler_params=pltpu.CompilerParams(
            dimension_semantics=("arbitrary", "arbitrary", "arbitrary"),
            vmem_limit_bytes=VMEM_LIMIT),
        name="mla_attn",
    )(qt, qt, k, vt, planes)


def _regroup(src_ref, dst_sc, stage_sc, seq):
    chunk = 256
    assert [d for _, d in PATTERNS] == [1, 4, 16]

    def copy1(c, carry):
        rows = pl.ds(pl.multiple_of(chunk * c, chunk), chunk)
        dst_sc[0, rows, :] = src_ref[rows, :].astype(BF16)
        return carry
    lax.fori_loop(0, seq // chunk, copy1, 0)

    len4, len16 = seq // 4, seq // 16
    for r in range(4):
        def copy4(c, carry, r=r):
            dst = pl.ds(pl.multiple_of(r * len4 + chunk * c, chunk), chunk)
            rows = src_ref[pl.ds(r + 4 * chunk * c, chunk, stride=4), :]
            stage_sc[dst, :] = rows
            dst_sc[1, dst, :] = rows.astype(BF16)
            return carry
        lax.fori_loop(0, len4 // chunk, copy4, 0)

    for r in range(4):
        for q in range(4):
            def copy16(c, carry, r=r, q=q):
                dst = pl.ds(pl.multiple_of((r + 4 * q) * len16 + chunk * c, chunk), chunk)
                rows = stage_sc[pl.ds(r * len4 + q + 4 * chunk * c, chunk, stride=4), :]
                dst_sc[2, dst, :] = rows.astype(BF16)
                return carry
            lax.fori_loop(0, len16 // chunk, copy16, 0)


def _dilated_kernel(q_ref, k_ref, v_ref, g_ref, tbl_ref, o_ref,
                    kc_sc, vc_sc, stage_sc, o_sc, m_sc, l_sc, *, seq, scale):
    tq = q_ref.shape[0]
    step = pl.program_id(2)

    @pl.when(step == 0)
    def _():
        _regroup(k_ref, kc_sc, stage_sc, seq)
        _regroup(v_ref, vc_sc, stage_sc, seq)

    t0 = step * tq
    n_units = tq // SUB
    for g, (_, d) in enumerate(PATTERNS):
        cls_len = seq // d
        sb_bits = (n_units // d).bit_length() - 1

        def unit(u, carry, g=g, d=d, cls_len=cls_len, sb_bits=sb_bits):
            r = lax.shift_right_logical(u, sb_bits)
            sb = u & ((1 << sb_bits) - 1)
            row0 = r + d * SUB * sb
            rows = pl.ds(row0, SUB, stride=d) if d > 1 else pl.ds(pl.multiple_of(row0, SUB), SUB)
            qc = q_ref[rows, :].astype(BF16)
            i0 = t0 // d + sb * SUB
            ws = jnp.clip(i0 - HALF_WIN, 0, cls_len - WIN)
            var = (i0 - ws) // HALF_WIN
            win = pl.ds(pl.multiple_of(r * cls_len + ws, HALF_WIN), WIN)
            s = _nt_dot(qc, kc_sc[g, win, :]) * scale + tbl_ref[g, var]
            m = jnp.max(s, axis=-1, keepdims=True)
            p = jnp.exp2(s - m)
            v_ext = jnp.concatenate([vc_sc[g, win, :], jnp.ones((WIN, HEAD_DIM), BF16)], axis=1)
            o_ext = jnp.dot(p.astype(BF16), v_ext, preferred_element_type=F32)
            o_sc[g, rows, :] = o_ext[:, :HEAD_DIM]
            l_sc[g, rows, :] = o_ext[:, HEAD_DIM:]
            m_sc[g, rows, :] = jnp.broadcast_to(m, (SUB, HEAD_DIM))
            return carry

        lax.fori_loop(0, n_units, unit, 0, unroll=UNROLL_DIL)

    m_all = jnp.maximum(jnp.maximum(m_sc[0], m_sc[1]), m_sc[2])
    num = jnp.zeros((tq, HEAD_DIM), F32)
    tot = jnp.zeros((tq, HEAD_DIM), F32)
    for g in range(len(PATTERNS)):
        w = jnp.exp2(m_sc[g] - m_all)
        num = num + w * o_sc[g]
        tot = tot + w * l_sc[g]
    o_ref[...] = ((num / tot) * _silu(g_ref[...])).astype(BF16)


def _dilated_attn(planes, tbl, bsz, seq, heads):
    tq = TQ_DIL
    per_b = seq // tq
    n_pat = len(PATTERNS)
    kern = functools.partial(_dilated_kernel, seq=seq, scale=LOG2E * HEAD_DIM ** -0.5)
    return pl.pallas_call(
        kern,
        grid=(bsz, heads, per_b),
        in_specs=[
            pl.BlockSpec((None, tq, HEAD_DIM), lambda b, h, i: (2, b * per_b + i, h)),
            pl.BlockSpec((None, seq, HEAD_DIM), lambda b, h, i: (3, b, h)),
            pl.BlockSpec((None, seq, HEAD_DIM), lambda b, h, i: (4, b, h)),
            pl.BlockSpec((None, tq, HEAD_DIM), lambda b, h, i: (5, b * per_b + i, h)),
            pl.BlockSpec((None, n_pat, 3, SUB, WIN), lambda b, h, i: (h, 0, 0, 0, 0)),
        ],
        out_specs=pl.BlockSpec((tq, HEAD_DIM), lambda b, h, i: (b * per_b + i, h)),
        out_shape=jax.ShapeDtypeStruct((bsz * seq, heads * HEAD_DIM), BF16),
        scratch_shapes=[
            pltpu.VMEM((n_pat, seq, HEAD_DIM), BF16),
            pltpu.VMEM((n_pat, seq, HEAD_DIM), BF16),
            pltpu.VMEM((seq, HEAD_DIM), F32),
            pltpu.VMEM((n_pat, tq, HEAD_DIM), F32),
            pltpu.VMEM((n_pat, tq, HEAD_DIM), F32),
            pltpu.VMEM((n_pat, tq, HEAD_DIM), F32),
        ],
        compiler_params=pltpu.CompilerParams(
            dimension_semantics=("arbitrary", "arbitrary", "arbitrary"),
            vmem_limit_bytes=VMEM_LIMIT),
        name="dilated_attn",
    )(planes, planes, planes, planes, tbl)


def _out_proj_kernel(ya_ref, yb_ref, x_ref, mod_ref, w_ref, fg_ref, o_ref, *, final):
    half = ya_ref.shape[1]
    y = (jnp.dot(ya_ref[...], w_ref[0:half, :], preferred_element_type=F32)
         + jnp.dot(yb_ref[...], w_ref[half:, :], preferred_element_type=F32))
    xn = x_ref[...] + mod_ref[2:3, :] * y
    if final:
        xn = _rms(xn, fg_ref[...])
    o_ref[...] = xn


def _out_proj(ya, yb, x2, mod, w_out, fg, layer, seq, final, in_place):
    rows, d = x2.shape
    tm = TM_OUT
    per_b = seq // tm
    half = ya.shape[1]
    kern = functools.partial(_out_proj_kernel, final=final)
    return pl.pallas_call(
        kern,
        grid=(rows // tm,),
        in_specs=[
            pl.BlockSpec((tm, half), lambda i: (i, 0)),
            pl.BlockSpec((tm, half), lambda i: (i, 0)),
            pl.BlockSpec((tm, d), lambda i: (i, 0)),
            pl.BlockSpec((None, None, 3, d), lambda i: (layer, i // per_b, 0, 0)),
            pl.BlockSpec((None,) + w_out.shape[1:], lambda i: (layer, 0, 0)),
            pl.BlockSpec((1, d), lambda i: (0, 0)),
        ],
        out_specs=pl.BlockSpec((tm, d), lambda i: (i, 0)),
        out_shape=jax.ShapeDtypeStruct((rows, d), F32),
        input_output_aliases={2: 0} if in_place else {},
        compiler_params=pltpu.CompilerParams(
            dimension_semantics=("arbitrary",), vmem_limit_bytes=VMEM_LIMIT),
        name="out_proj",
    )(ya, yb, x2, mod, w_out, fg)


def _t5_buckets(rel):
    nb = NUM_BUCKETS // 2
    max_exact = nb // 2
    base = np.where(rel > 0, nb, 0)
    n = np.abs(rel)
    large = max_exact + (np.log(np.maximum(n, 1) / max_exact)
                         / math.log(MAX_DISTANCE / max_exact) * (nb - max_exact)).astype(np.int32)
    large = np.minimum(large, nb - 1)
    return (base + np.where(n < max_exact, n, large)).astype(np.int32)


def _bias_tables(rel_bias):
    heads = rel_bias.shape[1]
    n = WIN + SUB
    pos = np.arange(n)
    delta = np.where(pos < WIN, pos, pos - n)
    rows = []
    for _, d in PATTERNS:
        for var in range(3):
            j = delta - var * HALF_WIN
            band = np.abs(j) <= HALF_WIN
            rows.append(np.where(band, _t5_buckets(np.where(band, j, 0) * d), NUM_BUCKETS))
    idx = np.stack(rows).reshape(-1)
    ext = jnp.concatenate([rel_bias.astype(F32) * LOG2E, jnp.full((1, heads), NEG, F32)], axis=0)
    gen = jnp.take(ext, jnp.asarray(idx), axis=0).reshape(len(rows), n, heads).transpose(2, 0, 1)
    tab = jnp.tile(gen, (1, 1, SUB))[:, :, :SUB * (n - 1)].reshape(heads, len(rows), SUB, n - 1)
    return tab[..., :WIN].reshape(heads, len(PATTERNS), 3, SUB, WIN)


def _rope_tables(positions):
    inv = 1.0 / (ROPE_THETA ** (jnp.arange(0, ROPE_DIM, 2, dtype=F32) / ROPE_DIM))
    ang = positions.astype(F32)[..., None] * inv
    cos, sin = jnp.cos(ang), jnp.sin(ang)
    ctab = jnp.concatenate([cos, cos, -sin, sin], axis=-1)
    return cos.swapaxes(1, 2), sin.swapaxes(1, 2), ctab


def kernel(x, c, positions, norm_g, ada_w, ada_b, w_in, q_a_norm_g, w_q_up, kv_a_norm_g,
           w_kv_up, rel_bias, w_out, final_norm_g):
    bsz, seq, d = x.shape
    depth = w_in.shape[0]
    heads = w_q_up.shape[2] // (HEAD_DIM + ROPE_DIM)
    width = heads * HEAD_DIM

    cq_w = w_in[:, :, 0:Q_LORA]
    ckv_w = w_in[:, :, Q_LORA:Q_LORA + KV_LORA]
    kr_w = w_in[:, :, Q_LORA + KV_LORA:Q_LORA + KV_LORA + ROPE_DIM]
    k1, k2 = kr_w[..., :HALF_ROPE], kr_w[..., HALF_ROPE:]
    pad = jnp.zeros((depth, d, GROUP - Q_LORA - KV_LORA - 2 * ROPE_DIM), w_in.dtype)
    w_lat = jnp.concatenate([cq_w, ckv_w, k1, k2, k2, k1, pad], axis=-1).astype(BF16)
    w_rest = w_in[:, :, Q_LORA + KV_LORA + ROPE_DIM:].astype(BF16)

    wq = w_q_up.reshape(depth, Q_LORA, heads, HEAD_DIM + ROPE_DIM)
    wq = jnp.pad(wq, ((0, 0), (0, 0), (0, 0), (0, QK_PAD - HEAD_DIM - ROPE_DIM)))
    wqt = wq.reshape(depth, Q_LORA, heads * QK_PAD).swapaxes(1, 2).astype(BF16)
    wkv = w_kv_up.reshape(depth, KV_LORA, heads, 2 * HEAD_DIM)
    wk = wkv[..., :HEAD_DIM].reshape(depth, KV_LORA, width).astype(BF16)
    wvt = wkv[..., HEAD_DIM:].reshape(depth, KV_LORA, width).swapaxes(1, 2).astype(BF16)
    w_out_b = w_out.astype(BF16)

    cos_t, sin_t, ctab = _rope_tables(positions)
    tbl = _bias_tables(rel_bias)

    c8 = jnp.pad(c, ((0, 8 - bsz), (0, 0)))
    mod = _adaln(c8, ada_w, ada_b)[:, :bsz].reshape(depth, bsz, 3, d)

    x2 = x.reshape(bsz * seq, d)
    fg = final_norm_g.reshape(1, d)
    norm_g3 = norm_g.reshape(depth, 1, d)
    qg3 = q_a_norm_g.reshape(depth, 1, Q_LORA)
    kvg3 = kv_a_norm_g.reshape(depth, 1, KV_LORA)
    for l in range(depth):
        planes = _in_proj(x2, mod, norm_g3, w_lat, w_rest, l, seq)
        qt, k, vt = _mla_prep(planes, cos_t, sin_t, ctab, qg3, kvg3, wqt, wk, wvt, l,
                              bsz, seq, heads)
        ya = _mla_attn(qt, k, vt, planes, bsz, seq, heads)
        yb = _dilated_attn(planes, tbl, bsz, seq, heads)
        x2 = _out_proj(ya, yb, x2, mod, w_out_b, fg, l, seq, final=(l == depth - 1),
                       in_place=(l > 0))
    return x2.reshape(bsz, seq, d)
```

```python
import functools
import math

import numpy as np
import jax
import jax.numpy as jnp
from jax import lax
from jax.experimental import pallas as pl
from jax.experimental.pallas import tpu as pltpu

F32 = jnp.float32
BF16 = jnp.bfloat16

HEAD_DIM = 128
ROPE_DIM = 64
HALF_ROPE = ROPE_DIM // 2
QK_PAD = 256
V_ROWS = HEAD_DIM + 16
Q_LORA = 512
KV_LORA = 256
GROUP = 1024
N_GROUPS = 6
PATTERNS = ((128, 1), (512, 4), (2048, 16))
HALF_WIN = 64
NUM_BUCKETS = 32
MAX_DISTANCE = 1024
ROPE_THETA = 10000.0
EPS = 1e-6
NEG = -1e30
LOG2E = math.log2(math.e)
MLA_EXP_SCALE = LOG2E / math.sqrt(HEAD_DIM + ROPE_DIM)

VMEM_LIMIT = 56 * 1024 * 1024

TM_IN = 1024
NORM_CHUNKS = 4
TS_PREP = 512
TQ_MLA = 1024
TK_MLA = 512
STEPS_MLA = 8
TQ_DIL = 2048
UNROLL_DIL = 16
TM_OUT = 512
SUB = 128
WIN = 256


def _nt_dot(a, b):
    return lax.dot_general(a, b, (((1,), (1,)), ((), ())), preferred_element_type=F32)


def _rms(x, g):
    return x * lax.rsqrt(jnp.mean(x * x, axis=-1, keepdims=True) + EPS) * g


def _silu(x):
    return x * jax.nn.sigmoid(x)


def _adaln_kernel(c_ref, w_ref, b_ref, o_ref):
    ca = _silu(c_ref[...]).astype(BF16)
    o_ref[...] = jnp.dot(ca, w_ref[...].astype(BF16), preferred_element_type=F32) + b_ref[...]


def _adaln(c8, ada_w, ada_b):
    depth, d, n3 = ada_w.shape
    tn = 768
    return pl.pallas_call(
        _adaln_kernel,
        grid=(depth, n3 // tn),
        in_specs=[
            pl.BlockSpec((8, d), lambda l, j: (0, 0)),
            pl.BlockSpec((None, d, tn), lambda l, j: (l, 0, j)),
            pl.BlockSpec((None, 1, tn), lambda l, j: (l, 0, j)),
        ],
        out_specs=pl.BlockSpec((None, 8, tn), lambda l, j: (l, 0, j)),
        out_shape=jax.ShapeDtypeStruct((depth, 8, n3), F32),
        compiler_params=pltpu.CompilerParams(
            dimension_semantics=("arbitrary", "arbitrary"), vmem_limit_bytes=VMEM_LIMIT),
        name="adaln",
    )(c8, ada_w, ada_b.reshape(depth, 1, n3))


def _in_proj_kernel(x_ref, mod_ref, g_ref, wlat_ref, w_ref, o_ref, h_sc):
    first = pl.program_id(1) == 0

    @pl.when(first)
    def _():
        chunk = x_ref.shape[0] // NORM_CHUNKS
        for c in range(NORM_CHUNKS):
            rows = slice(c * chunk, (c + 1) * chunk)
            y = _rms(x_ref[rows, :], g_ref[...])
            h = (y * (1.0 + mod_ref[1:2, :]) + mod_ref[0:1, :]).astype(BF16)
            h_sc[rows, :] = h
            o_ref[rows, :] = jnp.dot(h, wlat_ref[...], preferred_element_type=F32)

    @pl.when(jnp.logical_not(first))
    def _():
        o_ref[...] = jnp.dot(h_sc[...], w_ref[...], preferred_element_type=F32)


def _in_proj(x2, mod, norm_g, w_lat, w_rest, layer, seq):
    rows, d = x2.shape
    tm = TM_IN
    per_b = seq // tm
    return pl.pallas_call(
        _in_proj_kernel,
        grid=(rows // tm, N_GROUPS),
        in_specs=[
            pl.BlockSpec((tm, d), lambda i, j: (i, 0)),
            pl.BlockSpec((None, None, 3, d), lambda i, j: (layer, i // per_b, 0, 0)),
            pl.BlockSpec((None, 1, d), lambda i, j: (layer, 0, 0)),
            pl.BlockSpec((None, d, GROUP), lambda i, j: (layer, 0, 0)),
            pl.BlockSpec((None, d, GROUP), lambda i, j: (layer, 0, jnp.maximum(j, 1))),
        ],
        out_specs=pl.BlockSpec((None, tm, GROUP), lambda i, j: (j, i, 0)),
        out_shape=jax.ShapeDtypeStruct((N_GROUPS, rows, GROUP), F32),
        scratch_shapes=[pltpu.VMEM((tm, d), BF16)],
        compiler_params=pltpu.CompilerParams(
            dimension_semantics=("arbitrary", "arbitrary"), vmem_limit_bytes=VMEM_LIMIT),
        name="in_proj",
    )(x2, mod, norm_g, w_lat, w_rest)


def _mla_prep_kernel(lat_ref, cos_ref, sin_ref, ctab_ref, qg_ref, kvg_ref,
                     wqt_ref, wk_ref, wvt_ref, qt_ref, k_ref, vt_ref, *, heads):
    ts = lat_ref.shape[0]
    cqn = _rms(lat_ref[:, 0:Q_LORA], qg_ref[...]).astype(BF16)
    ckvn = _rms(lat_ref[:, Q_LORA:Q_LORA + KV_LORA], kvg_ref[...]).astype(BF16)

    qt = _nt_dot(wqt_ref[...], cqn)
    cos_t = cos_ref[...]
    sin_t = sin_ref[...]
    for h in range(heads):
        base = h * QK_PAD
        qt_ref[h, 0:HEAD_DIM, :] = (qt[base:base + HEAD_DIM] * MLA_EXP_SCALE).astype(BF16)
        t1 = qt[base + HEAD_DIM:base + HEAD_DIM + HALF_ROPE]
        t2 = qt[base + HEAD_DIM + HALF_ROPE:base + HEAD_DIM + ROPE_DIM]
        qt_ref[h, HEAD_DIM:HEAD_DIM + HALF_ROPE, :] = (
            (t1 * cos_t - t2 * sin_t) * MLA_EXP_SCALE).astype(BF16)
        qt_ref[h, HEAD_DIM + HALF_ROPE:HEAD_DIM + ROPE_DIM, :] = (
            (t2 * cos_t + t1 * sin_t) * MLA_EXP_SCALE).astype(BF16)
        qt_ref[h, HEAD_DIM + ROPE_DIM:QK_PAD, :] = jnp.zeros((QK_PAD - HEAD_DIM - ROPE_DIM, ts), BF16)

    y = lat_ref[:, Q_LORA + KV_LORA:Q_LORA + KV_LORA + 128] * ctab_ref[...]
    folded = y + pltpu.roll(y, 64, axis=1)
    lane = lax.broadcasted_iota(jnp.int32, folded.shape, 1)
    k_rope = jnp.where(lane < ROPE_DIM, folded, 0.0).astype(BF16)

    kn = jnp.dot(ckvn, wk_ref[...], preferred_element_type=F32)
    vt = _nt_dot(wvt_ref[...], ckvn)
    for h in range(heads):
        k_ref[h, :, 0:HEAD_DIM] = kn[:, h * HEAD_DIM:(h + 1) * HEAD_DIM].astype(BF16)
        k_ref[h, :, HEAD_DIM:QK_PAD] = k_rope
        for j in range(ts // TK_MLA):
            vt_ref[h, j, 0:HEAD_DIM, :] = vt[h * HEAD_DIM:(h + 1) * HEAD_DIM,
                                             j * TK_MLA:(j + 1) * TK_MLA].astype(BF16)
            vt_ref[h, j, HEAD_DIM:V_ROWS, :] = jnp.ones((V_ROWS - HEAD_DIM, TK_MLA), BF16)


def _mla_prep(planes, cos_t, sin_t, ctab, qg, kvg, wqt, wk, wvt, layer, bsz, seq, heads):
    ts = TS_PREP
    per_b = seq // ts
    kern = functools.partial(_mla_prep_kernel, heads=heads)
    return pl.pallas_call(
        kern,
        grid=(bsz, per_b),
        in_specs=[
            pl.BlockSpec((None, ts, GROUP), lambda b, i: (0, b * per_b + i, 0)),
            pl.BlockSpec((None, HALF_ROPE, ts), lambda b, i: (b, 0, i)),
            pl.BlockSpec((None, HALF_ROPE, ts), lambda b, i: (b, 0, i)),
            pl.BlockSpec((None, ts, 128), lambda b, i: (b, i, 0)),
            pl.BlockSpec((None, 1, Q_LORA), lambda b, i: (layer, 0, 0)),
            pl.BlockSpec((None, 1, KV_LORA), lambda b, i: (layer, 0, 0)),
            pl.BlockSpec((None,) + wqt.shape[1:], lambda b, i: (layer, 0, 0)),
            pl.BlockSpec((None,) + wk.shape[1:], lambda b, i: (layer, 0, 0)),
            pl.BlockSpec((None,) + wvt.shape[1:], lambda b, i: (layer, 0, 0)),
        ],
        out_specs=[
            pl.BlockSpec((None, heads, QK_PAD, ts), lambda b, i: (b, 0, 0, i)),
            pl.BlockSpec((None, heads, ts, QK_PAD), lambda b, i: (b, 0, i, 0)),
            pl.BlockSpec((None, heads, ts // TK_MLA, V_ROWS, TK_MLA), lambda b, i: (b, 0, i, 0, 0)),
        ],
        out_shape=[
            jax.ShapeDtypeStruct((bsz, heads, QK_PAD, seq), BF16),
            jax.ShapeDtypeStruct((bsz, heads, seq, QK_PAD), BF16),
            jax.ShapeDtypeStruct((bsz, heads, seq // TK_MLA, V_ROWS, TK_MLA), BF16),
        ],
        compiler_params=pltpu.CompilerParams(
            dimension_semantics=("arbitrary", "arbitrary"), vmem_limit_bytes=VMEM_LIMIT),
        name="mla_prep",
    )(planes, cos_t, sin_t, ctab, qg, kvg, wqt, wk, wvt)


def _mla_attn_kernel(qt_ref, qtn_ref, k_ref, vt_ref, g_ref, o_ref, q_sc, s_sc, m_sc, acc_sc):
    n_kt = vt_ref.shape[0]
    m_sc[...] = jnp.full(m_sc.shape, -jnp.inf, F32)
    acc_sc[...] = jnp.zeros(acc_sc.shape, F32)
    q_sc[0] = qt_ref[...]
    q_sc[1] = qtn_ref[...]

    def scores(kt, which):
        k = k_ref[pl.ds(pl.multiple_of(kt * TK_MLA, TK_MLA), TK_MLA), :]
        return jnp.dot(k, q_sc[which], preferred_element_type=F32)

    @pl.when(pl.program_id(2) == 0)
    def _():
        s_sc[0] = scores(0, 0)

    def step(kt, slot):
        wrap = jnp.asarray(kt + 1 == n_kt).astype(jnp.int32)
        s_sc[1 - slot] = scores((kt + 1) * (1 - wrap), wrap)
        s = s_sc[slot]
        m_prev = m_sc[...]
        m_new = jnp.maximum(m_prev, jnp.max(s, axis=0, keepdims=True))
        alpha = jnp.exp2(m_prev - m_new)
        p = jnp.exp2(s - m_new)
        acc_sc[...] = alpha * acc_sc[...] + jnp.dot(vt_ref[kt], p.astype(BF16),
                                                    preferred_element_type=F32)
        m_sc[...] = m_new

    def body(i, carry):
        for u in range(STEPS_MLA):
            step(STEPS_MLA * i + u, u % 2)
        return carry

    lax.fori_loop(0, n_kt // STEPS_MLA, body, 0)
    o_t = acc_sc[0:HEAD_DIM, :] / acc_sc[HEAD_DIM:HEAD_DIM + 1, :]
    o_ref[...] = (o_t.T * _silu(g_ref[...])).astype(BF16)


def _mla_attn(qt, k, vt, planes, bsz, seq, heads):
    tq = TQ_MLA
    per_b = seq // tq
    n_kt = seq // TK_MLA
    return pl.pallas_call(
        _mla_attn_kernel,
        grid=(bsz, heads, per_b),
        in_specs=[
            pl.BlockSpec((None, None, QK_PAD, tq), lambda b, h, i: (b, h, 0, i)),
            pl.BlockSpec((None, None, QK_PAD, tq),
                         lambda b, h, i: (b, h, 0, jnp.minimum(i + 1, per_b - 1))),
            pl.BlockSpec((None, None, seq, QK_PAD), lambda b, h, i: (b, h, 0, 0)),
            pl.BlockSpec((None, None, n_kt, V_ROWS, TK_MLA), lambda b, h, i: (b, h, 0, 0, 0)),
            pl.BlockSpec((None, tq, HEAD_DIM), lambda b, h, i: (1, b * per_b + i, h)),
        ],
        out_specs=pl.BlockSpec((tq, HEAD_DIM), lambda b, h, i: (b * per_b + i, h)),
        out_shape=jax.ShapeDtypeStruct((bsz * seq, heads * HEAD_DIM), BF16),
        scratch_shapes=[pltpu.VMEM((2, QK_PAD, tq), BF16), pltpu.VMEM((2, TK_MLA, tq), F32),
                        pltpu.VMEM((1, tq), F32), pltpu.VMEM((V_ROWS, tq), F32)],
        compiler_params=pltpu.CompilerParams(
            dimension_semantics=("arbitrary", "arbitrary", "arbitrary"),
            vmem_limit_bytes=VMEM_LIMIT),
        name="mla_attn",
    )(qt, qt, k, vt, planes)


def _regroup(src_ref, dst_sc, stage_sc, seq):
    chunk = 256
    assert [d for _, d in PATTERNS] == [1, 4, 16]

    def copy1(c, carry):
        rows = pl.ds(pl.multiple_of(chunk * c, chunk), chunk)
        dst_sc[0, rows, :] = src_ref[rows, :].astype(BF16)
        return carry
    lax.fori_loop(0, seq // chunk, copy1, 0)

    len4, len16 = seq // 4, seq // 16
    for r in range(4):
        def copy4(c, carry, r=r):
            dst = pl.ds(pl.multiple_of(r * len4 + chunk * c, chunk), chunk)
            rows = src_ref[pl.ds(r + 4 * chunk * c, chunk, stride=4), :]
            stage_sc[dst, :] = rows
            dst_sc[1, dst, :] = rows.astype(BF16)
            return carry
        lax.fori_loop(0, len4 // chunk, copy4, 0)

    for r in range(4):
        for q in range(4):
            def copy16(c, carry, r=r, q=q):
                dst = pl.ds(pl.multiple_of((r + 4 * q) * len16 + chunk * c, chunk), chunk)
                rows = stage_sc[pl.ds(r * len4 + q + 4 * chunk * c, chunk, stride=4), :]
                dst_sc[2, dst, :] = rows.astype(BF16)
                return carry
            lax.fori_loop(0, len16 // chunk, copy16, 0)


def _dilated_kernel(q_ref, k_ref, v_ref, g_ref, tbl_ref, o_ref,
                    kc_sc, vc_sc, stage_sc, o_sc, m_sc, l_sc, *, seq, scale):
    tq = q_ref.shape[0]
    step = pl.program_id(2)

    @pl.when(step == 0)
    def _():
        _regroup(k_ref, kc_sc, stage_sc, seq)
        _regroup(v_ref, vc_sc, stage_sc, seq)

    t0 = step * tq
    n_units = tq // SUB
    for g, (_, d) in enumerate(PATTERNS):
        cls_len = seq // d
        sb_bits = (n_units // d).bit_length() - 1

        def unit(u, carry, g=g, d=d, cls_len=cls_len, sb_bits=sb_bits):
            r = lax.shift_right_logical(u, sb_bits)
            sb = u & ((1 << sb_bits) - 1)
            row0 = r + d * SUB * sb
            rows = pl.ds(row0, SUB, stride=d) if d > 1 else pl.ds(pl.multiple_of(row0, SUB), SUB)
            qc = q_ref[rows, :].astype(BF16)
            i0 = t0 // d + sb * SUB
            ws = jnp.clip(i0 - HALF_WIN, 0, cls_len - WIN)
            var = (i0 - ws) // HALF_WIN
            win = pl.ds(pl.multiple_of(r * cls_len + ws, HALF_WIN), WIN)
            s = _nt_dot(qc, kc_sc[g, win, :]) * scale + tbl_ref[g, var]
            m = jnp.max(s, axis=-1, keepdims=True)
            p = jnp.exp2(s - m)
            v_ext = jnp.concatenate([vc_sc[g, win, :], jnp.ones((WIN, HEAD_DIM), BF16)], axis=1)
            o_ext = jnp.dot(p.astype(BF16), v_ext, preferred_element_type=F32)
            o_sc[g, rows, :] = o_ext[:, :HEAD_DIM]
            l_sc[g, rows, :] = o_ext[:, HEAD_DIM:]
            m_sc[g, rows, :] = jnp.broadcast_to(m, (SUB, HEAD_DIM))
            return carry

        lax.fori_loop(0, n_units, unit, 0, unroll=UNROLL_DIL)

    m_all = jnp.maximum(jnp.maximum(m_sc[0], m_sc[1]), m_sc[2])
    num = jnp.zeros((tq, HEAD_DIM), F32)
    tot = jnp.zeros((tq, HEAD_DIM), F32)
    for g in range(len(PATTERNS)):
        w = jnp.exp2(m_sc[g] - m_all)
        num = num + w * o_sc[g]
        tot = tot + w * l_sc[g]
    o_ref[...] = ((num / tot) * _silu(g_ref[...])).astype(BF16)


def _dilated_attn(planes, tbl, bsz, seq, heads):
    tq = TQ_DIL
    per_b = seq // tq
    n_pat = len(PATTERNS)
    kern = functools.partial(_dilated_kernel, seq=seq, scale=LOG2E * HEAD_DIM ** -0.5)
    return pl.pallas_call(
        kern,
        grid=(bsz, heads, per_b),
        in_specs=[
            pl.BlockSpec((None, tq, HEAD_DIM), lambda b, h, i: (2, b * per_b + i, h)),
            pl.BlockSpec((None, seq, HEAD_DIM), lambda b, h, i: (3, b, h)),
            pl.BlockSpec((None, seq, HEAD_DIM), lambda b, h, i: (4, b, h)),
            pl.BlockSpec((None, tq, HEAD_DIM), lambda b, h, i: (5, b * per_b + i, h)),
            pl.BlockSpec((None, n_pat, 3, SUB, WIN), lambda b, h, i: (h, 0, 0, 0, 0)),
        ],
        out_specs=pl.BlockSpec((tq, HEAD_DIM), lambda b, h, i: (b * per_b + i, h)),
        out_shape=jax.ShapeDtypeStruct((bsz * seq, heads * HEAD_DIM), BF16),
        scratch_shapes=[
            pltpu.VMEM((n_pat, seq, HEAD_DIM), BF16),
            pltpu.VMEM((n_pat, seq, HEAD_DIM), BF16),
            pltpu.VMEM((seq, HEAD_DIM), F32),
            pltpu.VMEM((n_pat, tq, HEAD_DIM), F32),
            pltpu.VMEM((n_pat, tq, HEAD_DIM), F32),
            pltpu.VMEM((n_pat, tq, HEAD_DIM), F32),
        ],
        compiler_params=pltpu.CompilerParams(
            dimension_semantics=("arbitrary", "arbitrary", "arbitrary"),
            vmem_limit_bytes=VMEM_LIMIT),
        name="dilated_attn",
    )(planes, planes, planes, planes, tbl)


def _out_proj_kernel(ya_ref, yb_ref, x_ref, mod_ref, w_ref, fg_ref, o_ref, *, final):
    half = ya_ref.shape[1]
    y = (jnp.dot(ya_ref[...], w_ref[0:half, :], preferred_element_type=F32)
         + jnp.dot(yb_ref[...], w_ref[half:, :], preferred_element_type=F32))
    xn = x_ref[...] + mod_ref[2:3, :] * y
    if final:
        xn = _rms(xn, fg_ref[...])
    o_ref[...] = xn


def _out_proj(ya, yb, x2, mod, w_out, fg, layer, seq, final, in_place):
    rows, d = x2.shape
    tm = TM_OUT
    per_b = seq // tm
    half = ya.shape[1]
    kern = functools.partial(_out_proj_kernel, final=final)
    return pl.pallas_call(
        kern,
        grid=(rows // tm,),
        in_specs=[
            pl.BlockSpec((tm, half), lambda i: (i, 0)),
            pl.BlockSpec((tm, half), lambda i: (i, 0)),
            pl.BlockSpec((tm, d), lambda i: (i, 0)),
            pl.BlockSpec((None, None, 3, d), lambda i: (layer, i // per_b, 0, 0)),
            pl.BlockSpec((None,) + w_out.shape[1:], lambda i: (layer, 0, 0)),
            pl.BlockSpec((1, d), lambda i: (0, 0)),
        ],
        out_specs=pl.BlockSpec((tm, d), lambda i: (i, 0)),
        out_shape=jax.ShapeDtypeStruct((rows, d), F32),
        input_output_aliases={2: 0} if in_place else {},
        compiler_params=pltpu.CompilerParams(
            dimension_semantics=("arbitrary",), vmem_limit_bytes=VMEM_LIMIT),
        name="out_proj",
    )(ya, yb, x2, mod, w_out, fg)


def _t5_buckets(rel):
    nb = NUM_BUCKETS // 2
    max_exact = nb // 2
    base = np.where(rel > 0, nb, 0)
    n = np.abs(rel)
    large = max_exact + (np.log(np.maximum(n, 1) / max_exact)
                         / math.log(MAX_DISTANCE / max_exact) * (nb - max_exact)).astype(np.int32)
    large = np.minimum(large, nb - 1)
    return (base + np.where(n < max_exact, n, large)).astype(np.int32)


def _bias_tables(rel_bias):
    heads = rel_bias.shape[1]
    n = WIN + SUB
    pos = np.arange(n)
    delta = np.where(pos < WIN, pos, pos - n)
    rows = []
    for _, d in PATTERNS:
        for var in range(3):
            j = delta - var * HALF_WIN
            band = np.abs(j) <= HALF_WIN
            rows.append(np.where(band, _t5_buckets(np.where(band, j, 0) * d), NUM_BUCKETS))
    idx = np.stack(rows).reshape(-1)
    ext = jnp.concatenate([rel_bias.astype(F32) * LOG2E, jnp.full((1, heads), NEG, F32)], axis=0)
    gen = jnp.take(ext, jnp.asarray(idx), axis=0).reshape(len(rows), n, heads).transpose(2, 0, 1)
    tab = jnp.tile(gen, (1, 1, SUB))[:, :, :SUB * (n - 1)].reshape(heads, len(rows), SUB, n - 1)
    return tab[..., :WIN].reshape(heads, len(PATTERNS), 3, SUB, WIN)


def _rope_tables(positions):
    inv = 1.0 / (ROPE_THETA ** (jnp.arange(0, ROPE_DIM, 2, dtype=F32) / ROPE_DIM))
    ang = positions.astype(F32)[..., None] * inv
    cos, sin = jnp.cos(ang), jnp.sin(ang)
    ctab = jnp.concatenate([cos, cos, -sin, sin], axis=-1)
    return cos.swapaxes(1, 2), sin.swapaxes(1, 2), ctab


def kernel(x, c, positions, norm_g, ada_w, ada_b, w_in, q_a_norm_g, w_q_up, kv_a_norm_g,
           w_kv_up, rel_bias, w_out, final_norm_g):
    bsz, seq, d = x.shape
    depth = w_in.shape[0]
    heads = w_q_up.shape[2] // (HEAD_DIM + ROPE_DIM)
    width = heads * HEAD_DIM

    cq_w = w_in[:, :, 0:Q_LORA]
    ckv_w = w_in[:, :, Q_LORA:Q_LORA + KV_LORA]
    kr_w = w_in[:, :, Q_LORA + KV_LORA:Q_LORA + KV_LORA + ROPE_DIM]
    k1, k2 = kr_w[..., :HALF_ROPE], kr_w[..., HALF_ROPE:]
    pad = jnp.zeros((depth, d, GROUP - Q_LORA - KV_LORA - 2 * ROPE_DIM), w_in.dtype)
    w_lat = jnp.concatenate([cq_w, ckv_w, k1, k2, k2, k1, pad], axis=-1).astype(BF16)
    lat_cols = Q_LORA + KV_LORA + ROPE_DIM
    w_rest = jnp.pad(w_in, ((0, 0), (0, 0), (GROUP - lat_cols, 0))).astype(BF16)

    wq = w_q_up.reshape(depth, Q_LORA, heads, HEAD_DIM + ROPE_DIM)
    wq = jnp.pad(wq, ((0, 0), (0, 0), (0, 0), (0, QK_PAD - HEAD_DIM - ROPE_DIM)))
    wqt = wq.reshape(depth, Q_LORA, heads * QK_PAD).swapaxes(1, 2).astype(BF16)
    wkv = w_kv_up.reshape(depth, KV_LORA, heads, 2 * HEAD_DIM)
    wk = wkv[..., :HEAD_DIM].reshape(depth, KV_LORA, width).astype(BF16)
    wvt = wkv[..., HEAD_DIM:].reshape(depth, KV_LORA, width).swapaxes(1, 2).astype(BF16)
    w_out_b = w_out.astype(BF16)

    cos_t, sin_t, ctab = _rope_tables(positions)
    tbl = _bias_tables(rel_bias)

    c8 = jnp.pad(c, ((0, 8 - bsz), (0, 0)))
    mod = _adaln(c8, ada_w, ada_b)[:, :bsz].reshape(depth, bsz, 3, d)

    x2 = x.reshape(bsz * seq, d)
    fg = final_norm_g.reshape(1, d)
    norm_g3 = norm_g.reshape(depth, 1, d)
    qg3 = q_a_norm_g.reshape(depth, 1, Q_LORA)
    kvg3 = kv_a_norm_g.reshape(depth, 1, KV_LORA)
    for l in range(depth):
        planes = _in_proj(x2, mod, norm_g3, w_lat, w_rest, l, seq)
        qt, k, vt = _mla_prep(planes, cos_t, sin_t, ctab, qg3, kvg3, wqt, wk, wvt, l,
                              bsz, seq, heads)
        ya = _mla_attn(qt, k, vt, planes, bsz, seq, heads)
        yb = _dilated_attn(planes, tbl, bsz, seq, heads)
        x2 = _out_proj(ya, yb, x2, mod, w_out_b, fg, l, seq, final=(l == depth - 1),
                       in_place=(l > 0))
    return x2.reshape(bsz, seq, d)
```

```python
import functools
import math

import numpy as np
import jax
import jax.numpy as jnp
from jax import lax
from jax.experimental import pallas as pl
from jax.experimental.pallas import tpu as pltpu

F32 = jnp.float32
BF16 = jnp.bfloat16

HEAD_DIM = 128
ROPE_DIM = 64
HALF_ROPE = ROPE_DIM // 2
QK_PAD = 256
V_ROWS = HEAD_DIM + 16
Q_LORA = 512
KV_LORA = 256
GROUP = 1024
N_GROUPS = 6
PATTERNS = ((128, 1), (512, 4), (2048, 16))
HALF_WIN = 64
NUM_BUCKETS = 32
MAX_DISTANCE = 1024
ROPE_THETA = 10000.0
EPS = 1e-6
NEG = -1e30
LOG2E = math.log2(math.e)
MLA_EXP_SCALE = LOG2E / math.sqrt(HEAD_DIM + ROPE_DIM)

VMEM_LIMIT = 56 * 1024 * 1024

TM_IN = 1024
NORM_CHUNKS = 4
TS_PREP = 1024
TK_MLA = 512
STEPS_MLA = 8
TQ_DIL = 2048
UNROLL_DIL = 16
TM_OUT = 512
SUB = 128
WIN = 256


def _nt_dot(a, b):
    return lax.dot_general(a, b, (((1,), (1,)), ((), ())), preferred_element_type=F32)


def _rms(x, g):
    return x * lax.rsqrt(jnp.mean(x * x, axis=-1, keepdims=True) + EPS) * g


def _silu(x):
    return x * jax.nn.sigmoid(x)


def _adaln_kernel(c_ref, w_ref, b_ref, o_ref):
    ca = _silu(c_ref[...]).astype(BF16)
    o_ref[...] = jnp.dot(ca, w_ref[...].astype(BF16), preferred_element_type=F32) + b_ref[...]


def _adaln(c8, ada_w, ada_b):
    depth, d, n3 = ada_w.shape
    tn = 768
    return pl.pallas_call(
        _adaln_kernel,
        grid=(depth, n3 // tn),
        in_specs=[
            pl.BlockSpec((8, d), lambda l, j: (0, 0)),
            pl.BlockSpec((None, d, tn), lambda l, j: (l, 0, j)),
            pl.BlockSpec((None, 1, tn), lambda l, j: (l, 0, j)),
        ],
        out_specs=pl.BlockSpec((None, 8, tn), lambda l, j: (l, 0, j)),
        out_shape=jax.ShapeDtypeStruct((depth, 8, n3), F32),
        compiler_params=pltpu.CompilerParams(
            dimension_semantics=("arbitrary", "arbitrary"), vmem_limit_bytes=VMEM_LIMIT),
        name="adaln",
    )(c8, ada_w, ada_b.reshape(depth, 1, n3))


def _in_proj_kernel(x_ref, mod_ref, g_ref, wlat_ref, w_ref, o_ref, h_sc):
    first = pl.program_id(1) == 0

    @pl.when(first)
    def _():
        chunk = x_ref.shape[0] // NORM_CHUNKS
        for c in range(NORM_CHUNKS):
            rows = slice(c * chunk, (c + 1) * chunk)
            y = _rms(x_ref[rows, :], g_ref[...])
            h = (y * (1.0 + mod_ref[1:2, :]) + mod_ref[0:1, :]).astype(BF16)
            h_sc[rows, :] = h
            o_ref[rows, :] = jnp.dot(h, wlat_ref[...], preferred_element_type=F32)

    @pl.when(jnp.logical_not(first))
    def _():
        o_ref[...] = jnp.dot(h_sc[...], w_ref[...], preferred_element_type=F32)


def _in_proj(x2, mod, norm_g, w_lat, w_rest, layer, seq):
    rows, d = x2.shape
    tm = TM_IN
    per_b = seq // tm
    return pl.pallas_call(
        _in_proj_kernel,
        grid=(rows // tm, N_GROUPS),
        in_specs=[
            pl.BlockSpec((tm, d), lambda i, j: (i, 0)),
            pl.BlockSpec((None, None, 3, d), lambda i, j: (layer, i // per_b, 0, 0)),
            pl.BlockSpec((None, 1, d), lambda i, j: (layer, 0, 0)),
            pl.BlockSpec((None, d, GROUP), lambda i, j: (layer, 0, 0)),
            pl.BlockSpec((None, d, GROUP), lambda i, j: (layer, 0, jnp.maximum(j - 1, 0))),
        ],
        out_specs=pl.BlockSpec((None, tm, GROUP), lambda i, j: (j, i, 0)),
        out_shape=jax.ShapeDtypeStruct((N_GROUPS, rows, GROUP), F32),
        scratch_shapes=[pltpu.VMEM((tm, d), BF16)],
        compiler_params=pltpu.CompilerParams(
            dimension_semantics=("arbitrary", "arbitrary"), vmem_limit_bytes=VMEM_LIMIT),
        name="in_proj",
    )(x2, mod, norm_g, w_lat, w_rest)


def _mla_prep_kernel(lat_ref, cos_ref, sin_ref, ctab_ref, qg_ref, kvg_ref,
                     wqt_ref, wk_ref, wvt_ref, qt_ref, k_ref, vt_ref, *, heads):
    ts = lat_ref.shape[0]
    cqn = _rms(lat_ref[:, 0:Q_LORA], qg_ref[...]).astype(BF16)
    ckvn = _rms(lat_ref[:, Q_LORA:Q_LORA + KV_LORA], kvg_ref[...]).astype(BF16)

    qt = _nt_dot(wqt_ref[...], cqn)
    cos_t = cos_ref[...]
    sin_t = sin_ref[...]
    for h in range(heads):
        base = h * QK_PAD
        qt_ref[h, 0:HEAD_DIM, :] = (qt[base:base + HEAD_DIM] * MLA_EXP_SCALE).astype(BF16)
        t1 = qt[base + HEAD_DIM:base + HEAD_DIM + HALF_ROPE]
        t2 = qt[base + HEAD_DIM + HALF_ROPE:base + HEAD_DIM + ROPE_DIM]
        qt_ref[h, HEAD_DIM:HEAD_DIM + HALF_ROPE, :] = (
            (t1 * cos_t - t2 * sin_t) * MLA_EXP_SCALE).astype(BF16)
        qt_ref[h, HEAD_DIM + HALF_ROPE:HEAD_DIM + ROPE_DIM, :] = (
            (t2 * cos_t + t1 * sin_t) * MLA_EXP_SCALE).astype(BF16)
        qt_ref[h, HEAD_DIM + ROPE_DIM:QK_PAD, :] = jnp.zeros((QK_PAD - HEAD_DIM - ROPE_DIM, ts), BF16)

    y = lat_ref[:, Q_LORA + KV_LORA:Q_LORA + KV_LORA + 128] * ctab_ref[...]
    folded = y + pltpu.roll(y, 64, axis=1)
    lane = lax.broadcasted_iota(jnp.int32, folded.shape, 1)
    k_rope = jnp.where(lane < ROPE_DIM, folded, 0.0).astype(BF16)

    kn = jnp.dot(ckvn, wk_ref[...], preferred_element_type=F32)
    vt = _nt_dot(wvt_ref[...], ckvn)
    for h in range(heads):
        k_ref[h, :, 0:HEAD_DIM] = kn[:, h * HEAD_DIM:(h + 1) * HEAD_DIM].astype(BF16)
        k_ref[h, :, HEAD_DIM:QK_PAD] = k_rope
        for j in range(ts // TK_MLA):
            vt_ref[h, j, 0:HEAD_DIM, :] = vt[h * HEAD_DIM:(h + 1) * HEAD_DIM,
                                             j * TK_MLA:(j + 1) * TK_MLA].astype(BF16)
            vt_ref[h, j, HEAD_DIM:V_ROWS, :] = jnp.ones((V_ROWS - HEAD_DIM, TK_MLA), BF16)


def _mla_prep(planes, cos_t, sin_t, ctab, qg, kvg, wqt, wk, wvt, layer, bsz, seq, heads):
    ts = TS_PREP
    per_b = seq // ts
    kern = functools.partial(_mla_prep_kernel, heads=heads)
    return pl.pallas_call(
        kern,
        grid=(bsz, per_b),
        in_specs=[
            pl.BlockSpec((None, ts, GROUP), lambda b, i: (0, b * per_b + i, 0)),
            pl.BlockSpec((None, HALF_ROPE, ts), lambda b, i: (b, 0, i)),
            pl.BlockSpec((None, HALF_ROPE, ts), lambda b, i: (b, 0, i)),
            pl.BlockSpec((None, ts, 128), lambda b, i: (b, i, 0)),
            pl.BlockSpec((None, 1, Q_LORA), lambda b, i: (layer, 0, 0)),
            pl.BlockSpec((None, 1, KV_LORA), lambda b, i: (layer, 0, 0)),
            pl.BlockSpec((None,) + wqt.shape[1:], lambda b, i: (layer, 0, 0)),
            pl.BlockSpec((None,) + wk.shape[1:], lambda b, i: (layer, 0, 0)),
            pl.BlockSpec((None,) + wvt.shape[1:], lambda b, i: (layer, 0, 0)),
        ],
        out_specs=[
            pl.BlockSpec((None, heads, None, QK_PAD, ts), lambda b, i: (b, 0, i, 0, 0)),
            pl.BlockSpec((None, heads, ts, QK_PAD), lambda b, i: (b, 0, i, 0)),
            pl.BlockSpec((None, heads, ts // TK_MLA, V_ROWS, TK_MLA), lambda b, i: (b, 0, i, 0, 0)),
        ],
        out_shape=[
            jax.ShapeDtypeStruct((bsz, heads, per_b, QK_PAD, ts), BF16),
            jax.ShapeDtypeStruct((bsz, heads, seq, QK_PAD), BF16),
            jax.ShapeDtypeStruct((bsz, heads, seq // TK_MLA, V_ROWS, TK_MLA), BF16),
        ],
        compiler_params=pltpu.CompilerParams(
            dimension_semantics=("arbitrary", "arbitrary"), vmem_limit_bytes=VMEM_LIMIT),
        name="mla_prep",
    )(planes, cos_t, sin_t, ctab, qg, kvg, wqt, wk, wvt)


def _mla_attn_kernel(qt_ref, k_ref, vt_ref, g_ref, o_ref, s_sc, m_sc, acc_sc):
    n_q, _, tq = qt_ref.shape
    n_kt = vt_ref.shape[0]

    def scores(kt, qi):
        k = k_ref[pl.ds(pl.multiple_of(kt * TK_MLA, TK_MLA), TK_MLA), :]
        return jnp.dot(k, qt_ref[qi], preferred_element_type=F32)

    s_sc[0] = scores(0, 0)

    def q_tile(qi, carry):
        m_sc[...] = jnp.full(m_sc.shape, -jnp.inf, F32)
        acc_sc[...] = jnp.zeros(acc_sc.shape, F32)

        def step(kt, slot):
            wrap = jnp.asarray(kt + 1 == n_kt).astype(jnp.int32)
            s_sc[1 - slot] = scores((kt + 1) * (1 - wrap), jnp.minimum(qi + wrap, n_q - 1))
            s = s_sc[slot]
            m_prev = m_sc[...]
            m_new = jnp.maximum(m_prev, jnp.max(s, axis=0, keepdims=True))
            alpha = jnp.exp2(m_prev - m_new)
            p = jnp.exp2(s - m_new)
            acc_sc[...] = alpha * acc_sc[...] + jnp.dot(vt_ref[kt], p.astype(BF16),
                                                        preferred_element_type=F32)
            m_sc[...] = m_new

        def body(i, c):
            for u in range(STEPS_MLA):
                step(STEPS_MLA * i + u, u % 2)
            return c

        lax.fori_loop(0, n_kt // STEPS_MLA, body, 0)
        rows = pl.ds(pl.multiple_of(qi * tq, tq), tq)
        o_t = acc_sc[0:HEAD_DIM, :] / acc_sc[HEAD_DIM:HEAD_DIM + 1, :]
        o_ref[rows, :] = (o_t.T * _silu(g_ref[rows, :])).astype(BF16)
        return carry

    lax.fori_loop(0, n_q, q_tile, 0)


def _mla_attn(qt, k, vt, planes, bsz, seq, heads):
    n_q, tq = qt.shape[2], qt.shape[4]
    n_kt = seq // TK_MLA
    return pl.pallas_call(
        _mla_attn_kernel,
        grid=(bsz, heads),
        in_specs=[
            pl.BlockSpec((None, None, n_q, QK_PAD, tq), lambda b, h: (b, h, 0, 0, 0)),
            pl.BlockSpec((None, None, seq, QK_PAD), lambda b, h: (b, h, 0, 0)),
            pl.BlockSpec((None, None, n_kt, V_ROWS, TK_MLA), lambda b, h: (b, h, 0, 0, 0)),
            pl.BlockSpec((None, seq, HEAD_DIM), lambda b, h: (1, b, h)),
        ],
        out_specs=pl.BlockSpec((seq, HEAD_DIM), lambda b, h: (b, h)),
        out_shape=jax.ShapeDtypeStruct((bsz * seq, heads * HEAD_DIM), BF16),
        scratch_shapes=[pltpu.VMEM((2, TK_MLA, tq), F32), pltpu.VMEM((1, tq), F32),
                        pltpu.VMEM((V_ROWS, tq), F32)],
        compiler_params=pltpu.CompilerParams(
            dimension_semantics=("arbitrary", "arbitrary"), vmem_limit_bytes=VMEM_LIMIT),
        name="mla_attn",
    )(qt, k, vt, planes)


def _regroup(src_ref, dst_sc, stage_sc, seq):
    chunk = 256
    assert [d for _, d in PATTERNS] == [1, 4, 16]

    def copy1(c, carry):
        rows = pl.ds(pl.multiple_of(chunk * c, chunk), chunk)
        dst_sc[0, rows, :] = src_ref[rows, :].astype(BF16)
        return carry
    lax.fori_loop(0, seq // chunk, copy1, 0)

    len4, len16 = seq // 4, seq // 16
    for r in range(4):
        def copy4(c, carry, r=r):
            dst = pl.ds(pl.multiple_of(r * len4 + chunk * c, chunk), chunk)
            rows = src_ref[pl.ds(r + 4 * chunk * c, chunk, stride=4), :]
            stage_sc[dst, :] = rows
            dst_sc[1, dst, :] = rows.astype(BF16)
            return carry
        lax.fori_loop(0, len4 // chunk, copy4, 0)

    for r in range(4):
        for q in range(4):
            def copy16(c, carry, r=r, q=q):
                dst = pl.ds(pl.multiple_of((r + 4 * q) * len16 + chunk * c, chunk), chunk)
                rows = stage_sc[pl.ds(r * len4 + q + 4 * chunk * c, chunk, stride=4), :]
                dst_sc[2, dst, :] = rows.astype(BF16)
                return carry
            lax.fori_loop(0, len16 // chunk, copy16, 0)


def _dilated_kernel(q_ref, k_ref, v_ref, g_ref, tbl_ref, o_ref,
                    kc_sc, vc_sc, stage_sc, o_sc, m_sc, l_sc, *, seq, scale):
    tq = q_ref.shape[0]
    step = pl.program_id(2)

    @pl.when(step == 0)
    def _():
        _regroup(k_ref, kc_sc, stage_sc, seq)
        _regroup(v_ref, vc_sc, stage_sc, seq)

    t0 = step * tq
    n_units = tq // SUB
    for g, (_, d) in enumerate(PATTERNS):
        cls_len = seq // d
        sb_bits = (n_units // d).bit_length() - 1

        def unit(u, carry, g=g, d=d, cls_len=cls_len, sb_bits=sb_bits):
            r = lax.shift_right_logical(u, sb_bits)
            sb = u & ((1 << sb_bits) - 1)
            row0 = r + d * SUB * sb
            rows = pl.ds(row0, SUB, stride=d) if d > 1 else pl.ds(pl.multiple_of(row0, SUB), SUB)
            qc = q_ref[rows, :].astype(BF16)
            i0 = t0 // d + sb * SUB
            ws = jnp.clip(i0 - HALF_WIN, 0, cls_len - WIN)
            var = (i0 - ws) // HALF_WIN
            win = pl.ds(pl.multiple_of(r * cls_len + ws, HALF_WIN), WIN)
            s = _nt_dot(qc, kc_sc[g, win, :]) * scale + tbl_ref[g, var]
            m = jnp.max(s, axis=-1, keepdims=True)
            p = jnp.exp2(s - m)
            v_ext = jnp.concatenate([vc_sc[g, win, :], jnp.ones((WIN, HEAD_DIM), BF16)], axis=1)
            o_ext = jnp.dot(p.astype(BF16), v_ext, preferred_element_type=F32)
            o_sc[g, rows, :] = o_ext[:, :HEAD_DIM]
            l_sc[g, rows, :] = o_ext[:, HEAD_DIM:]
            m_sc[g, rows, :] = jnp.broadcast_to(m, (SUB, HEAD_DIM))
            return carry

        lax.fori_loop(0, n_units, unit, 0, unroll=UNROLL_DIL)

    m_all = jnp.maximum(jnp.maximum(m_sc[0], m_sc[1]), m_sc[2])
    num = jnp.zeros((tq, HEAD_DIM), F32)
    tot = jnp.zeros((tq, HEAD_DIM), F32)
    for g in range(len(PATTERNS)):
        w = jnp.exp2(m_sc[g] - m_all)
        num = num + w * o_sc[g]
        tot = tot + w * l_sc[g]
    o_ref[...] = ((num / tot) * _silu(g_ref[...])).astype(BF16)


def _dilated_attn(planes, tbl, bsz, seq, heads):
    tq = TQ_DIL
    per_b = seq // tq
    n_pat = len(PATTERNS)
    kern = functools.partial(_dilated_kernel, seq=seq, scale=LOG2E * HEAD_DIM ** -0.5)
    return pl.pallas_call(
        kern,
        grid=(bsz, heads, per_b),
        in_specs=[
            pl.BlockSpec((None, tq, HEAD_DIM), lambda b, h, i: (2, b * per_b + i, h)),
            pl.BlockSpec((None, seq, HEAD_DIM), lambda b, h, i: (3, b, h)),
            pl.BlockSpec((None, seq, HEAD_DIM), lambda b, h, i: (4, b, h)),
            pl.BlockSpec((None, tq, HEAD_DIM), lambda b, h, i: (5, b * per_b + i, h)),
            pl.BlockSpec((None, n_pat, 3, SUB, WIN), lambda b, h, i: (h, 0, 0, 0, 0)),
        ],
        out_specs=pl.BlockSpec((tq, HEAD_DIM), lambda b, h, i: (b * per_b + i, h)),
        out_shape=jax.ShapeDtypeStruct((bsz * seq, heads * HEAD_DIM), BF16),
        scratch_shapes=[
            pltpu.VMEM((n_pat, seq, HEAD_DIM), BF16),
            pltpu.VMEM((n_pat, seq, HEAD_DIM), BF16),
            pltpu.VMEM((seq, HEAD_DIM), F32),
            pltpu.VMEM((n_pat, tq, HEAD_DIM), F32),
            pltpu.VMEM((n_pat, tq, HEAD_DIM), F32),
            pltpu.VMEM((n_pat, tq, HEAD_DIM), F32),
        ],
        compiler_params=pltpu.CompilerParams(
            dimension_semantics=("arbitrary", "arbitrary", "arbitrary"),
            vmem_limit_bytes=VMEM_LIMIT),
        name="dilated_attn",
    )(planes, planes, planes, planes, tbl)


def _out_proj_kernel(ya_ref, yb_ref, x_ref, mod_ref, w_ref, fg_ref, o_ref, *, final):
    half = ya_ref.shape[1]
    y = (jnp.dot(ya_ref[...], w_ref[0:half, :], preferred_element_type=F32)
         + jnp.dot(yb_ref[...], w_ref[half:, :], preferred_element_type=F32))
    xn = x_ref[...] + mod_ref[2:3, :] * y
    if final:
        xn = _rms(xn, fg_ref[...])
    o_ref[...] = xn


def _out_proj(ya, yb, x2, mod, w_out, fg, layer, seq, final, in_place):
    rows, d = x2.shape
    tm = TM_OUT
    per_b = seq // tm
    half = ya.shape[1]
    kern = functools.partial(_out_proj_kernel, final=final)
    return pl.pallas_call(
        kern,
        grid=(rows // tm,),
        in_specs=[
            pl.BlockSpec((tm, half), lambda i: (i, 0)),
            pl.BlockSpec((tm, half), lambda i: (i, 0)),
            pl.BlockSpec((tm, d), lambda i: (i, 0)),
            pl.BlockSpec((None, None, 3, d), lambda i: (layer, i // per_b, 0, 0)),
            pl.BlockSpec((None,) + w_out.shape[1:], lambda i: (layer, 0, 0)),
            pl.BlockSpec((1, d), lambda i: (0, 0)),
        ],
        out_specs=pl.BlockSpec((tm, d), lambda i: (i, 0)),
        out_shape=jax.ShapeDtypeStruct((rows, d), F32),
        input_output_aliases={2: 0} if in_place else {},
        compiler_params=pltpu.CompilerParams(
            dimension_semantics=("arbitrary",), vmem_limit_bytes=VMEM_LIMIT),
        name="out_proj",
    )(ya, yb, x2, mod, w_out, fg)


def _t5_buckets(rel):
    nb = NUM_BUCKETS // 2
    max_exact = nb // 2
    base = np.where(rel > 0, nb, 0)
    n = np.abs(rel)
    large = max_exact + (np.log(np.maximum(n, 1) / max_exact)
                         / math.log(MAX_DISTANCE / max_exact) * (nb - max_exact)).astype(np.int32)
    large = np.minimum(large, nb - 1)
    return (base + np.where(n < max_exact, n, large)).astype(np.int32)


def _bias_tables(rel_bias):
    heads = rel_bias.shape[1]
    n = WIN + SUB
    pos = np.arange(n)
    delta = np.where(pos < WIN, pos, pos - n)
    rows = []
    for _, d in PATTERNS:
        for var in range(3):
            j = delta - var * HALF_WIN
            band = np.abs(j) <= HALF_WIN
            rows.append(np.where(band, _t5_buckets(np.where(band, j, 0) * d), NUM_BUCKETS))
    idx = np.stack(rows).reshape(-1)
    ext = jnp.concatenate([rel_bias.astype(F32) * LOG2E, jnp.full((1, heads), NEG, F32)], axis=0)
    gen = jnp.take(ext, jnp.asarray(idx), axis=0).reshape(len(rows), n, heads).transpose(2, 0, 1)
    tab = jnp.tile(gen, (1, 1, SUB))[:, :, :SUB * (n - 1)].reshape(heads, len(rows), SUB, n - 1)
    return tab[..., :WIN].reshape(heads, len(PATTERNS), 3, SUB, WIN)


def _rope_tables(positions):
    inv = 1.0 / (ROPE_THETA ** (jnp.arange(0, ROPE_DIM, 2, dtype=F32) / ROPE_DIM))
    ang = positions.astype(F32)[..., None] * inv
    cos, sin = jnp.cos(ang), jnp.sin(ang)
    ctab = jnp.concatenate([cos, cos, -sin, sin], axis=-1)
    return cos.swapaxes(1, 2), sin.swapaxes(1, 2), ctab


def kernel(x, c, positions, norm_g, ada_w, ada_b, w_in, q_a_norm_g, w_q_up, kv_a_norm_g,
           w_kv_up, rel_bias, w_out, final_norm_g):
    bsz, seq, d = x.shape
    depth = w_in.shape[0]
    heads = w_q_up.shape[2] // (HEAD_DIM + ROPE_DIM)
    width = heads * HEAD_DIM

    cq_w = w_in[:, :, 0:Q_LORA]
    ckv_w = w_in[:, :, Q_LORA:Q_LORA + KV_LORA]
    kr_w = w_in[:, :, Q_LORA + KV_LORA:Q_LORA + KV_LORA + ROPE_DIM]
    k1, k2 = kr_w[..., :HALF_ROPE], kr_w[..., HALF_ROPE:]
    pad = jnp.zeros((depth, d, GROUP - Q_LORA - KV_LORA - 2 * ROPE_DIM), w_in.dtype)
    w_lat = jnp.concatenate([cq_w, ckv_w, k1, k2, k2, k1, pad], axis=-1).astype(BF16)
    w_rest = w_in[:, :, Q_LORA + KV_LORA + ROPE_DIM:].astype(BF16)

    wq = w_q_up.reshape(depth, Q_LORA, heads, HEAD_DIM + ROPE_DIM)
    wq = jnp.pad(wq, ((0, 0), (0, 0), (0, 0), (0, QK_PAD - HEAD_DIM - ROPE_DIM)))
    wqt = wq.reshape(depth, Q_LORA, heads * QK_PAD).swapaxes(1, 2).astype(BF16)
    wkv = w_kv_up.reshape(depth, KV_LORA, heads, 2 * HEAD_DIM)
    wk = wkv[..., :HEAD_DIM].reshape(depth, KV_LORA, width).astype(BF16)
    wvt = wkv[..., HEAD_DIM:].reshape(depth, KV_LORA, width).swapaxes(1, 2).astype(BF16)
    w_out_b = w_out.astype(BF16)

    cos_t, sin_t, ctab = _rope_tables(positions)
    tbl = _bias_tables(rel_bias)

    c8 = jnp.pad(c, ((0, 8 - bsz), (0, 0)))
    mod = _adaln(c8, ada_w, ada_b)[:, :bsz].reshape(depth, bsz, 3, d)

    x2 = x.reshape(bsz * seq, d)
    fg = final_norm_g.reshape(1, d)
    norm_g3 = norm_g.reshape(depth, 1, d)
    qg3 = q_a_norm_g.reshape(depth, 1, Q_LORA)
    kvg3 = kv_a_norm_g.reshape(depth, 1, KV_LORA)
    for l in range(depth):
        planes = _in_proj(x2, mod, norm_g3, w_lat, w_rest, l, seq)
        qt, k, vt = _mla_prep(planes, cos_t, sin_t, ctab, qg3, kvg3, wqt, wk, wvt, l,
                              bsz, seq, heads)
        ya = _mla_attn(qt, k, vt, planes, bsz, seq, heads)
        yb = _dilated_attn(planes, tbl, bsz, seq, heads)
        x2 = _out_proj(ya, yb, x2, mod, w_out_b, fg, l, seq, final=(l == depth - 1),
                       in_place=(l > 0))
    return x2.reshape(bsz, seq, d)
```

```python
import functools
import math

import numpy as np
import jax
import jax.numpy as jnp
from jax import lax
from jax.experimental import pallas as pl
from jax.experimental.pallas import tpu as pltpu

F32 = jnp.float32
BF16 = jnp.bfloat16

HEAD_DIM = 128
ROPE_DIM = 64
HALF_ROPE = ROPE_DIM // 2
QK_PAD = 256
V_ROWS = HEAD_DIM + 16
Q_LORA = 512
KV_LORA = 256
GROUP = 1024
N_GROUPS = 6
PATTERNS = ((128, 1), (512, 4), (2048, 16))
HALF_WIN = 64
NUM_BUCKETS = 32
MAX_DISTANCE = 1024
ROPE_THETA = 10000.0
EPS = 1e-6
NEG = -1e30
LOG2E = math.log2(math.e)
MLA_EXP_SCALE = LOG2E / math.sqrt(HEAD_DIM + ROPE_DIM)

VMEM_LIMIT = 56 * 1024 * 1024

TM_IN = 1024
NORM_CHUNKS = 4
TS_PREP = 1024
TK_MLA = 512
STEPS_MLA = 8
TQ_DIL = 2048
UNROLL_DIL = 16
TM_OUT = 512
SUB = 128
WIN = 256


def _nt_dot(a, b):
    return lax.dot_general(a, b, (((1,), (1,)), ((), ())), preferred_element_type=F32)


def _rms(x, g):
    return x * lax.rsqrt(jnp.mean(x * x, axis=-1, keepdims=True) + EPS) * g


def _silu(x):
    return x * jax.nn.sigmoid(x)


def _adaln_kernel(c_ref, w_ref, b_ref, o_ref):
    ca = _silu(c_ref[...]).astype(BF16)
    o_ref[...] = jnp.dot(ca, w_ref[...].astype(BF16), preferred_element_type=F32) + b_ref[...]


def _adaln(c8, ada_w, ada_b):
    depth, d, n3 = ada_w.shape
    tn = 768
    return pl.pallas_call(
        _adaln_kernel,
        grid=(depth, n3 // tn),
        in_specs=[
            pl.BlockSpec((8, d), lambda l, j: (0, 0)),
            pl.BlockSpec((None, d, tn), lambda l, j: (l, 0, j)),
            pl.BlockSpec((None, 1, tn), lambda l, j: (l, 0, j)),
        ],
        out_specs=pl.BlockSpec((None, 8, tn), lambda l, j: (l, 0, j)),
        out_shape=jax.ShapeDtypeStruct((depth, 8, n3), F32),
        compiler_params=pltpu.CompilerParams(
            dimension_semantics=("arbitrary", "arbitrary"), vmem_limit_bytes=VMEM_LIMIT),
        name="adaln",
    )(c8, ada_w, ada_b.reshape(depth, 1, n3))


def _in_proj_kernel(x_ref, mod_ref, g_ref, wlat_ref, w_ref, o_ref, h_sc):
    first = pl.program_id(1) == 0

    @pl.when(first)
    def _():
        chunk = x_ref.shape[0] // NORM_CHUNKS
        for c in range(NORM_CHUNKS):
            rows = slice(c * chunk, (c + 1) * chunk)
            y = _rms(x_ref[rows, :], g_ref[...])
            h = (y * (1.0 + mod_ref[1:2, :]) + mod_ref[0:1, :]).astype(BF16)
            h_sc[rows, :] = h
            o_ref[rows, :] = jnp.dot(h, wlat_ref[...], preferred_element_type=F32)

    @pl.when(jnp.logical_not(first))
    def _():
        o_ref[...] = jnp.dot(h_sc[...], w_ref[...], preferred_element_type=F32)


def _in_proj(x2, mod, norm_g, w_lat, w_rest, layer, seq):
    rows, d = x2.shape
    tm = TM_IN
    per_b = seq // tm
    return pl.pallas_call(
        _in_proj_kernel,
        grid=(rows // tm, N_GROUPS),
        in_specs=[
            pl.BlockSpec((tm, d), lambda i, j: (i, 0)),
            pl.BlockSpec((None, None, 3, d), lambda i, j: (layer, i // per_b, 0, 0)),
            pl.BlockSpec((None, 1, d), lambda i, j: (layer, 0, 0)),
            pl.BlockSpec((None, d, GROUP), lambda i, j: (layer, 0, 0)),
            pl.BlockSpec((None, d, GROUP), lambda i, j: (layer, 0, jnp.maximum(j - 1, 0))),
        ],
        out_specs=pl.BlockSpec((None, tm, GROUP), lambda i, j: (j, i, 0)),
        out_shape=jax.ShapeDtypeStruct((N_GROUPS, rows, GROUP), F32),
        scratch_shapes=[pltpu.VMEM((tm, d), BF16)],
        compiler_params=pltpu.CompilerParams(
            dimension_semantics=("arbitrary", "arbitrary"), vmem_limit_bytes=VMEM_LIMIT),
        name="in_proj",
    )(x2, mod, norm_g, w_lat, w_rest)


def _mla_prep_kernel(lat_ref, cos_ref, sin_ref, ctab_ref, qg_ref, kvg_ref,
                     wqt_ref, wk_ref, wvt_ref, qt_ref, k_ref, vt_ref, *, heads):
    ts = lat_ref.shape[0]
    cqn = _rms(lat_ref[:, 0:Q_LORA], qg_ref[...]).astype(BF16)
    ckvn = _rms(lat_ref[:, Q_LORA:Q_LORA + KV_LORA], kvg_ref[...]).astype(BF16)

    qt = _nt_dot(wqt_ref[...], cqn)
    cos_t = cos_ref[...]
    sin_t = sin_ref[...]
    for h in range(heads):
        base = h * QK_PAD
        qt_ref[h, 0:HEAD_DIM, :] = (qt[base:base + HEAD_DIM] * MLA_EXP_SCALE).astype(BF16)
        t1 = qt[base + HEAD_DIM:base + HEAD_DIM + HALF_ROPE]
        t2 = qt[base + HEAD_DIM + HALF_ROPE:base + HEAD_DIM + ROPE_DIM]
        qt_ref[h, HEAD_DIM:HEAD_DIM + HALF_ROPE, :] = (
            (t1 * cos_t - t2 * sin_t) * MLA_EXP_SCALE).astype(BF16)
        qt_ref[h, HEAD_DIM + HALF_ROPE:HEAD_DIM + ROPE_DIM, :] = (
            (t2 * cos_t + t1 * sin_t) * MLA_EXP_SCALE).astype(BF16)
        qt_ref[h, HEAD_DIM + ROPE_DIM:QK_PAD, :] = jnp.zeros((QK_PAD - HEAD_DIM - ROPE_DIM, ts), BF16)

    y = lat_ref[:, Q_LORA + KV_LORA:Q_LORA + KV_LORA + 128] * ctab_ref[...]
    folded = y + pltpu.roll(y, 64, axis=1)
    lane = lax.broadcasted_iota(jnp.int32, folded.shape, 1)
    k_rope = jnp.where(lane < ROPE_DIM, folded, 0.0).astype(BF16)

    kn = jnp.dot(ckvn, wk_ref[...], preferred_element_type=F32)
    vt = _nt_dot(wvt_ref[...], ckvn)
    for h in range(heads):
        k_ref[h, :, 0:HEAD_DIM] = kn[:, h * HEAD_DIM:(h + 1) * HEAD_DIM].astype(BF16)
        k_ref[h, :, HEAD_DIM:QK_PAD] = k_rope
        for j in range(ts // TK_MLA):
            vt_ref[h, j, 0:HEAD_DIM, :] = vt[h * HEAD_DIM:(h + 1) * HEAD_DIM,
                                             j * TK_MLA:(j + 1) * TK_MLA].astype(BF16)
            vt_ref[h, j, HEAD_DIM:V_ROWS, :] = jnp.ones((V_ROWS - HEAD_DIM, TK_MLA), BF16)


def _mla_prep(planes, cos_t, sin_t, ctab, qg, kvg, wqt, wk, wvt, layer, bsz, seq, heads):
    ts = TS_PREP
    per_b = seq // ts
    kern = functools.partial(_mla_prep_kernel, heads=heads)
    return pl.pallas_call(
        kern,
        grid=(bsz, per_b),
        in_specs=[
            pl.BlockSpec((None, ts, GROUP), lambda b, i: (0, b * per_b + i, 0)),
            pl.BlockSpec((None, HALF_ROPE, ts), lambda b, i: (b, 0, i)),
            pl.BlockSpec((None, HALF_ROPE, ts), lambda b, i: (b, 0, i)),
            pl.BlockSpec((None, ts, 128), lambda b, i: (b, i, 0)),
            pl.BlockSpec((None, 1, Q_LORA), lambda b, i: (layer, 0, 0)),
            pl.BlockSpec((None, 1, KV_LORA), lambda b, i: (layer, 0, 0)),
            pl.BlockSpec((None,) + wqt.shape[1:], lambda b, i: (layer, 0, 0)),
            pl.BlockSpec((None,) + wk.shape[1:], lambda b, i: (layer, 0, 0)),
            pl.BlockSpec((None,) + wvt.shape[1:], lambda b, i: (layer, 0, 0)),
        ],
        out_specs=[
            pl.BlockSpec((None, heads, None, QK_PAD, ts), lambda b, i: (b, 0, i, 0, 0)),
            pl.BlockSpec((None, heads, ts, QK_PAD), lambda b, i: (b, 0, i, 0)),
            pl.BlockSpec((None, heads, ts // TK_MLA, V_ROWS, TK_MLA), lambda b, i: (b, 0, i, 0, 0)),
        ],
        out_shape=[
            jax.ShapeDtypeStruct((bsz, heads, per_b, QK_PAD, ts), BF16),
            jax.ShapeDtypeStruct((bsz, heads, seq, QK_PAD), BF16),
            jax.ShapeDtypeStruct((bsz, heads, seq // TK_MLA, V_ROWS, TK_MLA), BF16),
        ],
        compiler_params=pltpu.CompilerParams(
            dimension_semantics=("arbitrary", "arbitrary"), vmem_limit_bytes=VMEM_LIMIT),
        name="mla_prep",
    )(planes, cos_t, sin_t, ctab, qg, kvg, wqt, wk, wvt)


def _mla_attn_kernel(qt_ref, k_ref, vt_ref, g_ref, o_ref, s_sc, m_sc, acc_sc, accp_sc):
    n_q, _, tq = qt_ref.shape
    n_kt = vt_ref.shape[0]
    n_parts = n_kt // STEPS_MLA
    part = tq // n_parts
    assert accp_sc.shape == (n_parts, V_ROWS, part)

    def finish(acc, row0):
        rows = pl.ds(pl.multiple_of(row0, part), part)
        o_t = acc[0:HEAD_DIM, :] / acc[HEAD_DIM:HEAD_DIM + 1, :]
        o_ref[rows, :] = (o_t.T * _silu(g_ref[rows, :])).astype(BF16)

    accp_sc[...] = jnp.ones(accp_sc.shape, F32)

    def scores(kt, qi):
        k = k_ref[pl.ds(pl.multiple_of(kt * TK_MLA, TK_MLA), TK_MLA), :]
        return jnp.dot(k, qt_ref[qi], preferred_element_type=F32)

    s_sc[0] = scores(0, 0)

    def q_tile(qi, carry):
        m_sc[...] = jnp.full(m_sc.shape, -jnp.inf, F32)
        acc_sc[...] = jnp.zeros(acc_sc.shape, F32)

        def step(kt, slot):
            wrap = jnp.asarray(kt + 1 == n_kt).astype(jnp.int32)
            s_sc[1 - slot] = scores((kt + 1) * (1 - wrap), jnp.minimum(qi + wrap, n_q - 1))
            s = s_sc[slot]
            m_prev = m_sc[...]
            m_new = jnp.maximum(m_prev, jnp.max(s, axis=0, keepdims=True))
            alpha = jnp.exp2(m_prev - m_new)
            p = jnp.exp2(s - m_new)
            acc_sc[...] = alpha * acc_sc[...] + jnp.dot(vt_ref[kt], p.astype(BF16),
                                                        preferred_element_type=F32)
            m_sc[...] = m_new

        prev = jnp.maximum(qi - 1, 0)

        def body(i, c):
            for u in range(STEPS_MLA):
                step(STEPS_MLA * i + u, u % 2)
            finish(accp_sc[i], prev * tq + i * part)
            return c

        lax.fori_loop(0, n_parts, body, 0)
        for j in range(n_parts):
            accp_sc[j] = acc_sc[:, j * part:(j + 1) * part]
        return carry

    lax.fori_loop(0, n_q, q_tile, 0)
    for j in range(n_parts):
        finish(accp_sc[j], (n_q - 1) * tq + j * part)


def _mla_attn(qt, k, vt, planes, bsz, seq, heads):
    n_q, tq = qt.shape[2], qt.shape[4]
    n_kt = seq // TK_MLA
    return pl.pallas_call(
        _mla_attn_kernel,
        grid=(bsz, heads),
        in_specs=[
            pl.BlockSpec((None, None, n_q, QK_PAD, tq), lambda b, h: (b, h, 0, 0, 0)),
            pl.BlockSpec((None, None, seq, QK_PAD), lambda b, h: (b, h, 0, 0)),
            pl.BlockSpec((None, None, n_kt, V_ROWS, TK_MLA), lambda b, h: (b, h, 0, 0, 0)),
            pl.BlockSpec((None, seq, HEAD_DIM), lambda b, h: (1, b, h)),
        ],
        out_specs=pl.BlockSpec((seq, HEAD_DIM), lambda b, h: (b, h)),
        out_shape=jax.ShapeDtypeStruct((bsz * seq, heads * HEAD_DIM), BF16),
        scratch_shapes=[pltpu.VMEM((2, TK_MLA, tq), F32), pltpu.VMEM((1, tq), F32),
                        pltpu.VMEM((V_ROWS, tq), F32),
                        pltpu.VMEM((n_kt // STEPS_MLA, V_ROWS, tq * STEPS_MLA // n_kt), F32)],
        compiler_params=pltpu.CompilerParams(
            dimension_semantics=("arbitrary", "arbitrary"), vmem_limit_bytes=VMEM_LIMIT),
        name="mla_attn",
    )(qt, k, vt, planes)


def _regroup(src_ref, dst_sc, stage_sc, seq):
    chunk = 256
    assert [d for _, d in PATTERNS] == [1, 4, 16]

    def copy1(c, carry):
        rows = pl.ds(pl.multiple_of(chunk * c, chunk), chunk)
        dst_sc[0, rows, :] = src_ref[rows, :].astype(BF16)
        return carry
    lax.fori_loop(0, seq // chunk, copy1, 0)

    len4, len16 = seq // 4, seq // 16
    for r in range(4):
        def copy4(c, carry, r=r):
            dst = pl.ds(pl.multiple_of(r * len4 + chunk * c, chunk), chunk)
            rows = src_ref[pl.ds(r + 4 * chunk * c, chunk, stride=4), :]
            stage_sc[dst, :] = rows
            dst_sc[1, dst, :] = rows.astype(BF16)
            return carry
        lax.fori_loop(0, len4 // chunk, copy4, 0)

    for r in range(4):
        for q in range(4):
            def copy16(c, carry, r=r, q=q):
                dst = pl.ds(pl.multiple_of((r + 4 * q) * len16 + chunk * c, chunk), chunk)
                rows = stage_sc[pl.ds(r * len4 + q + 4 * chunk * c, chunk, stride=4), :]
                dst_sc[2, dst, :] = rows.astype(BF16)
                return carry
            lax.fori_loop(0, len16 // chunk, copy16, 0)


def _dilated_kernel(q_ref, k_ref, v_ref, g_ref, tbl_ref, o_ref,
                    kc_sc, vc_sc, stage_sc, o_sc, m_sc, l_sc, *, seq, scale):
    tq = q_ref.shape[0]
    step = pl.program_id(2)

    @pl.when(step == 0)
    def _():
        _regroup(k_ref, kc_sc, stage_sc, seq)
        _regroup(v_ref, vc_sc, stage_sc, seq)

    t0 = step * tq
    n_units = tq // SUB
    for g, (_, d) in enumerate(PATTERNS):
        cls_len = seq // d
        sb_bits = (n_units // d).bit_length() - 1

        def unit(u, carry, g=g, d=d, cls_len=cls_len, sb_bits=sb_bits):
            r = lax.shift_right_logical(u, sb_bits)
            sb = u & ((1 << sb_bits) - 1)
            row0 = r + d * SUB * sb
            rows = pl.ds(row0, SUB, stride=d) if d > 1 else pl.ds(pl.multiple_of(row0, SUB), SUB)
            qc = q_ref[rows, :].astype(BF16)
            i0 = t0 // d + sb * SUB
            ws = jnp.clip(i0 - HALF_WIN, 0, cls_len - WIN)
            var = (i0 - ws) // HALF_WIN
            win = pl.ds(pl.multiple_of(r * cls_len + ws, HALF_WIN), WIN)
            s = _nt_dot(qc, kc_sc[g, win, :]) * scale + tbl_ref[g, var]
            m = jnp.max(s, axis=-1, keepdims=True)
            p = jnp.exp2(s - m)
            v_ext = jnp.concatenate([vc_sc[g, win, :], jnp.ones((WIN, HEAD_DIM), BF16)], axis=1)
            o_ext = jnp.dot(p.astype(BF16), v_ext, preferred_element_type=F32)
            o_sc[g, rows, :] = o_ext[:, :HEAD_DIM]
            l_sc[g, rows, :] = o_ext[:, HEAD_DIM:]
            m_sc[g, rows, :] = jnp.broadcast_to(m, (SUB, HEAD_DIM))
            return carry

        lax.fori_loop(0, n_units, unit, 0, unroll=UNROLL_DIL)

    m_all = jnp.maximum(jnp.maximum(m_sc[0], m_sc[1]), m_sc[2])
    num = jnp.zeros((tq, HEAD_DIM), F32)
    tot = jnp.zeros((tq, HEAD_DIM), F32)
    for g in range(len(PATTERNS)):
        w = jnp.exp2(m_sc[g] - m_all)
        num = num + w * o_sc[g]
        tot = tot + w * l_sc[g]
    o_ref[...] = ((num / tot) * _silu(g_ref[...])).astype(BF16)


def _dilated_attn(planes, tbl, bsz, seq, heads):
    tq = TQ_DIL
    per_b = seq // tq
    n_pat = len(PATTERNS)
    kern = functools.partial(_dilated_kernel, seq=seq, scale=LOG2E * HEAD_DIM ** -0.5)
    return pl.pallas_call(
        kern,
        grid=(bsz, heads, per_b),
        in_specs=[
            pl.BlockSpec((None, tq, HEAD_DIM), lambda b, h, i: (2, b * per_b + i, h)),
            pl.BlockSpec((None, seq, HEAD_DIM), lambda b, h, i: (3, b, h)),
            pl.BlockSpec((None, seq, HEAD_DIM), lambda b, h, i: (4, b, h)),
            pl.BlockSpec((None, tq, HEAD_DIM), lambda b, h, i: (5, b * per_b + i, h)),
            pl.BlockSpec((None, n_pat, 3, SUB, WIN), lambda b, h, i: (h, 0, 0, 0, 0)),
        ],
        out_specs=pl.BlockSpec((tq, HEAD_DIM), lambda b, h, i: (b * per_b + i, h)),
        out_shape=jax.ShapeDtypeStruct((bsz * seq, heads * HEAD_DIM), BF16),
        scratch_shapes=[
            pltpu.VMEM((n_pat, seq, HEAD_DIM), BF16),
            pltpu.VMEM((n_pat, seq, HEAD_DIM), BF16),
            pltpu.VMEM((seq, HEAD_DIM), F32),
            pltpu.VMEM((n_pat, tq, HEAD_DIM), F32),
            pltpu.VMEM((n_pat, tq, HEAD_DIM), F32),
            pltpu.VMEM((n_pat, tq, HEAD_DIM), F32),
        ],
        compiler_params=pltpu.CompilerParams(
            dimension_semantics=("arbitrary", "arbitrary", "arbitrary"),
            vmem_limit_bytes=VMEM_LIMIT),
        name="dilated_attn",
    )(planes, planes, planes, planes, tbl)


def _out_proj_kernel(ya_ref, yb_ref, x_ref, mod_ref, w_ref, fg_ref, o_ref, *, final):
    half = ya_ref.shape[1]
    y = (jnp.dot(ya_ref[...], w_ref[0:half, :], preferred_element_type=F32)
         + jnp.dot(yb_ref[...], w_ref[half:, :], preferred_element_type=F32))
    xn = x_ref[...] + mod_ref[2:3, :] * y
    if final:
        xn = _rms(xn, fg_ref[...])
    o_ref[...] = xn


def _out_proj(ya, yb, x2, mod, w_out, fg, layer, seq, final, in_place):
    rows, d = x2.shape
    tm = TM_OUT
    per_b = seq // tm
    half = ya.shape[1]
    kern = functools.partial(_out_proj_kernel, final=final)
    return pl.pallas_call(
        kern,
        grid=(rows // tm,),
        in_specs=[
            pl.BlockSpec((tm, half), lambda i: (i, 0)),
            pl.BlockSpec((tm, half), lambda i: (i, 0)),
            pl.BlockSpec((tm, d), lambda i: (i, 0)),
            pl.BlockSpec((None, None, 3, d), lambda i: (layer, i // per_b, 0, 0)),
            pl.BlockSpec((None,) + w_out.shape[1:], lambda i: (layer, 0, 0)),
            pl.BlockSpec((1, d), lambda i: (0, 0)),
        ],
        out_specs=pl.BlockSpec((tm, d), lambda i: (i, 0)),
        out_shape=jax.ShapeDtypeStruct((rows, d), F32),
        input_output_aliases={2: 0} if in_place else {},
        compiler_params=pltpu.CompilerParams(
            dimension_semantics=("arbitrary",), vmem_limit_bytes=VMEM_LIMIT),
        name="out_proj",
    )(ya, yb, x2, mod, w_out, fg)


def _t5_buckets(rel):
    nb = NUM_BUCKETS // 2
    max_exact = nb // 2
    base = np.where(rel > 0, nb, 0)
    n = np.abs(rel)
    large = max_exact + (np.log(np.maximum(n, 1) / max_exact)
                         / math.log(MAX_DISTANCE / max_exact) * (nb - max_exact)).astype(np.int32)
    large = np.minimum(large, nb - 1)
    return (base + np.where(n < max_exact, n, large)).astype(np.int32)


def _bias_tables(rel_bias):
    heads = rel_bias.shape[1]
    n = WIN + SUB
    pos = np.arange(n)
    delta = np.where(pos < WIN, pos, pos - n)
    rows = []
    for _, d in PATTERNS:
        for var in range(3):
            j = delta - var * HALF_WIN
            band = np.abs(j) <= HALF_WIN
            rows.append(np.where(band, _t5_buckets(np.where(band, j, 0) * d), NUM_BUCKETS))
    idx = np.stack(rows).reshape(-1)
    ext = jnp.concatenate([rel_bias.astype(F32) * LOG2E, jnp.full((1, heads), NEG, F32)], axis=0)
    gen = jnp.take(ext, jnp.asarray(idx), axis=0).reshape(len(rows), n, heads).transpose(2, 0, 1)
    tab = jnp.tile(gen, (1, 1, SUB))[:, :, :SUB * (n - 1)].reshape(heads, len(rows), SUB, n - 1)
    return tab[..., :WIN].reshape(heads, len(PATTERNS), 3, SUB, WIN)


def _rope_tables(positions):
    inv = 1.0 / (ROPE_THETA ** (jnp.arange(0, ROPE_DIM, 2, dtype=F32) / ROPE_DIM))
    ang = positions.astype(F32)[..., None] * inv
    cos, sin = jnp.cos(ang), jnp.sin(ang)
    ctab = jnp.concatenate([cos, cos, -sin, sin], axis=-1)
    return cos.swapaxes(1, 2), sin.swapaxes(1, 2), ctab


def kernel(x, c, positions, norm_g, ada_w, ada_b, w_in, q_a_norm_g, w_q_up, kv_a_norm_g,
           w_kv_up, rel_bias, w_out, final_norm_g):
    bsz, seq, d = x.shape
    depth = w_in.shape[0]
    heads = w_q_up.shape[2] // (HEAD_DIM + ROPE_DIM)
    width = heads * HEAD_DIM

    cq_w = w_in[:, :, 0:Q_LORA]
    ckv_w = w_in[:, :, Q_LORA:Q_LORA + KV_LORA]
    kr_w = w_in[:, :, Q_LORA + KV_LORA:Q_LORA + KV_LORA + ROPE_DIM]
    k1, k2 = kr_w[..., :HALF_ROPE], kr_w[..., HALF_ROPE:]
    pad = jnp.zeros((depth, d, GROUP - Q_LORA - KV_LORA - 2 * ROPE_DIM), w_in.dtype)
    w_lat = jnp.concatenate([cq_w, ckv_w, k1, k2, k2, k1, pad], axis=-1).astype(BF16)
    w_rest = w_in[:, :, Q_LORA + KV_LORA + ROPE_DIM:].astype(BF16)

    wq = w_q_up.reshape(depth, Q_LORA, heads, HEAD_DIM + ROPE_DIM)
    wq = jnp.pad(wq, ((0, 0), (0, 0), (0, 0), (0, QK_PAD - HEAD_DIM - ROPE_DIM)))
    wqt = wq.reshape(depth, Q_LORA, heads * QK_PAD).swapaxes(1, 2).astype(BF16)
    wkv = w_kv_up.reshape(depth, KV_LORA, heads, 2 * HEAD_DIM)
    wk = wkv[..., :HEAD_DIM].reshape(depth, KV_LORA, width).astype(BF16)
    wvt = wkv[..., HEAD_DIM:].reshape(depth, KV_LORA, width).swapaxes(1, 2).astype(BF16)
    w_out_b = w_out.astype(BF16)

    cos_t, sin_t, ctab = _rope_tables(positions)
    tbl = _bias_tables(rel_bias)

    c8 = jnp.pad(c, ((0, 8 - bsz), (0, 0)))
    mod = _adaln(c8, ada_w, ada_b)[:, :bsz].reshape(depth, bsz, 3, d)

    x2 = x.reshape(bsz * seq, d)
    fg = final_norm_g.reshape(1, d)
    norm_g3 = norm_g.reshape(depth, 1, d)
    qg3 = q_a_norm_g.reshape(depth, 1, Q_LORA)
    kvg3 = kv_a_norm_g.reshape(depth, 1, KV_LORA)
    for l in range(depth):
        planes = _in_proj(x2, mod, norm_g3, w_lat, w_rest, l, seq)
        qt, k, vt = _mla_prep(planes, cos_t, sin_t, ctab, qg3, kvg3, wqt, wk, wvt, l,
                              bsz, seq, heads)
        ya = _mla_attn(qt, k, vt, planes, bsz, seq, heads)
        yb = _dilated_attn(planes, tbl, bsz, seq, heads)
        x2 = _out_proj(ya, yb, x2, mod, w_out_b, fg, l, seq, final=(l == depth - 1),
                       in_place=(l > 0))
    return x2.reshape(bsz, seq, d)
```

```python
import functools
import math

import numpy as np
import jax
import jax.numpy as jnp
from jax import lax
from jax.experimental import pallas as pl
from jax.experimental.pallas import tpu as pltpu

F32 = jnp.float32
BF16 = jnp.bfloat16

HEAD_DIM = 128
ROPE_DIM = 64
HALF_ROPE = ROPE_DIM // 2
QK_PAD = 256
V_ROWS = HEAD_DIM + 16
Q_LORA = 512
KV_LORA = 256
GROUP = 1024
N_GROUPS = 6
PATTERNS = ((128, 1), (512, 4), (2048, 16))
HALF_WIN = 64
NUM_BUCKETS = 32
MAX_DISTANCE = 1024
ROPE_THETA = 10000.0
EPS = 1e-6
NEG = -1e30
LOG2E = math.log2(math.e)
MLA_EXP_SCALE = LOG2E / math.sqrt(HEAD_DIM + ROPE_DIM)

VMEM_LIMIT = 56 * 1024 * 1024

TM_IN = 1024
NORM_CHUNKS = 4
TS_PREP = 1024
TK_MLA = 512
STEPS_MLA = 8
TQ_DIL = 2048
UNROLL_DIL = 16
TM_OUT = 512
SUB = 128
WIN = 256


def _nt_dot(a, b):
    return lax.dot_general(a, b, (((1,), (1,)), ((), ())), preferred_element_type=F32)


def _rms(x, g):
    return x * lax.rsqrt(jnp.mean(x * x, axis=-1, keepdims=True) + EPS) * g


def _silu(x):
    return x * jax.nn.sigmoid(x)


def _adaln_kernel(c_ref, w_ref, b_ref, o_ref):
    ca = _silu(c_ref[...]).astype(BF16)
    o_ref[...] = jnp.dot(ca, w_ref[...].astype(BF16), preferred_element_type=F32) + b_ref[...]


def _adaln(c8, ada_w, ada_b):
    depth, d, n3 = ada_w.shape
    tn = 768
    return pl.pallas_call(
        _adaln_kernel,
        grid=(depth, n3 // tn),
        in_specs=[
            pl.BlockSpec((8, d), lambda l, j: (0, 0)),
            pl.BlockSpec((None, d, tn), lambda l, j: (l, 0, j)),
            pl.BlockSpec((None, 1, tn), lambda l, j: (l, 0, j)),
        ],
        out_specs=pl.BlockSpec((None, 8, tn), lambda l, j: (l, 0, j)),
        out_shape=jax.ShapeDtypeStruct((depth, 8, n3), F32),
        compiler_params=pltpu.CompilerParams(
            dimension_semantics=("arbitrary", "arbitrary"), vmem_limit_bytes=VMEM_LIMIT),
        name="adaln",
    )(c8, ada_w, ada_b.reshape(depth, 1, n3))


def _in_proj_kernel(x_ref, mod_ref, g_ref, wlat_ref, w_ref, o_ref, h_sc):
    first = pl.program_id(1) == 0

    @pl.when(first)
    def _():
        chunk = x_ref.shape[0] // NORM_CHUNKS
        for c in range(NORM_CHUNKS):
            rows = slice(c * chunk, (c + 1) * chunk)
            y = _rms(x_ref[rows, :], g_ref[...])
            h = (y * (1.0 + mod_ref[1:2, :]) + mod_ref[0:1, :]).astype(BF16)
            h_sc[rows, :] = h
            o_ref[rows, :] = jnp.dot(h, wlat_ref[...], preferred_element_type=F32)

    @pl.when(jnp.logical_not(first))
    def _():
        o_ref[...] = jnp.dot(h_sc[...], w_ref[...], preferred_element_type=F32)


def _in_proj(x2, mod, norm_g, w_lat, w_rest, layer, seq):
    rows, d = x2.shape
    tm = TM_IN
    per_b = seq // tm
    return pl.pallas_call(
        _in_proj_kernel,
        grid=(rows // tm, N_GROUPS),
        in_specs=[
            pl.BlockSpec((tm, d), lambda i, j: (i, 0)),
            pl.BlockSpec((None, None, 3, d), lambda i, j: (layer, i // per_b, 0, 0)),
            pl.BlockSpec((None, 1, d), lambda i, j: (layer, 0, 0)),
            pl.BlockSpec((None, d, GROUP), lambda i, j: (layer, 0, 0)),
            pl.BlockSpec((None, d, GROUP), lambda i, j: (layer, 0, jnp.maximum(j - 1, 0))),
        ],
        out_specs=pl.BlockSpec((None, tm, GROUP), lambda i, j: (j, i, 0)),
        out_shape=jax.ShapeDtypeStruct((N_GROUPS, rows, GROUP), F32),
        scratch_shapes=[pltpu.VMEM((tm, d), BF16)],
        compiler_params=pltpu.CompilerParams(
            dimension_semantics=("arbitrary", "arbitrary"), vmem_limit_bytes=VMEM_LIMIT),
        name="in_proj",
    )(x2, mod, norm_g, w_lat, w_rest)


def _mla_prep_kernel(lat_ref, cos_ref, sin_ref, ctab_ref, qg_ref, kvg_ref,
                     wqt_ref, wk_ref, wvt_ref, qt_ref, k_ref, vt_ref, *, heads):
    ts = lat_ref.shape[0]
    cqn = _rms(lat_ref[:, 0:Q_LORA], qg_ref[...]).astype(BF16)
    ckvn = _rms(lat_ref[:, Q_LORA:Q_LORA + KV_LORA], kvg_ref[...]).astype(BF16)

    qt = _nt_dot(wqt_ref[...], cqn)
    cos_t = cos_ref[...]
    sin_t = sin_ref[...]
    for h in range(heads):
        base = h * (HEAD_DIM + ROPE_DIM)
        qt_ref[h, 0:HEAD_DIM, :] = (qt[base:base + HEAD_DIM] * MLA_EXP_SCALE).astype(BF16)
        t1 = qt[base + HEAD_DIM:base + HEAD_DIM + HALF_ROPE]
        t2 = qt[base + HEAD_DIM + HALF_ROPE:base + HEAD_DIM + ROPE_DIM]
        qt_ref[h, HEAD_DIM:HEAD_DIM + HALF_ROPE, :] = (
            (t1 * cos_t - t2 * sin_t) * MLA_EXP_SCALE).astype(BF16)
        qt_ref[h, HEAD_DIM + HALF_ROPE:HEAD_DIM + ROPE_DIM, :] = (
            (t2 * cos_t + t1 * sin_t) * MLA_EXP_SCALE).astype(BF16)
        qt_ref[h, HEAD_DIM + ROPE_DIM:QK_PAD, :] = jnp.zeros((QK_PAD - HEAD_DIM - ROPE_DIM, ts), BF16)

    y = lat_ref[:, Q_LORA + KV_LORA:Q_LORA + KV_LORA + 128] * ctab_ref[...]
    folded = y + pltpu.roll(y, 64, axis=1)
    lane = lax.broadcasted_iota(jnp.int32, folded.shape, 1)
    k_rope = jnp.where(lane < ROPE_DIM, folded, 0.0).astype(BF16)

    kn = jnp.dot(ckvn, wk_ref[...], preferred_element_type=F32)
    vt = _nt_dot(wvt_ref[...], ckvn)
    for h in range(heads):
        k_ref[h, :, 0:HEAD_DIM] = kn[:, h * HEAD_DIM:(h + 1) * HEAD_DIM].astype(BF16)
        k_ref[h, :, HEAD_DIM:QK_PAD] = k_rope
        for j in range(ts // TK_MLA):
            vt_ref[h, j, 0:HEAD_DIM, :] = vt[h * HEAD_DIM:(h + 1) * HEAD_DIM,
                                             j * TK_MLA:(j + 1) * TK_MLA].astype(BF16)
            vt_ref[h, j, HEAD_DIM:V_ROWS, :] = jnp.ones((V_ROWS - HEAD_DIM, TK_MLA), BF16)


def _mla_prep(planes, cos_t, sin_t, ctab, qg, kvg, wqt, wk, wvt, layer, bsz, seq, heads):
    ts = TS_PREP
    per_b = seq // ts
    kern = functools.partial(_mla_prep_kernel, heads=heads)
    return pl.pallas_call(
        kern,
        grid=(bsz, per_b),
        in_specs=[
            pl.BlockSpec((None, ts, GROUP), lambda b, i: (0, b * per_b + i, 0)),
            pl.BlockSpec((None, HALF_ROPE, ts), lambda b, i: (b, 0, i)),
            pl.BlockSpec((None, HALF_ROPE, ts), lambda b, i: (b, 0, i)),
            pl.BlockSpec((None, ts, 128), lambda b, i: (b, i, 0)),
            pl.BlockSpec((None, 1, Q_LORA), lambda b, i: (layer, 0, 0)),
            pl.BlockSpec((None, 1, KV_LORA), lambda b, i: (layer, 0, 0)),
            pl.BlockSpec((None,) + wqt.shape[1:], lambda b, i: (layer, 0, 0)),
            pl.BlockSpec((None,) + wk.shape[1:], lambda b, i: (layer, 0, 0)),
            pl.BlockSpec((None,) + wvt.shape[1:], lambda b, i: (layer, 0, 0)),
        ],
        out_specs=[
            pl.BlockSpec((None, heads, None, QK_PAD, ts), lambda b, i: (b, 0, i, 0, 0)),
            pl.BlockSpec((None, heads, ts, QK_PAD), lambda b, i: (b, 0, i, 0)),
            pl.BlockSpec((None, heads, ts // TK_MLA, V_ROWS, TK_MLA), lambda b, i: (b, 0, i, 0, 0)),
        ],
        out_shape=[
            jax.ShapeDtypeStruct((bsz, heads, per_b, QK_PAD, ts), BF16),
            jax.ShapeDtypeStruct((bsz, heads, seq, QK_PAD), BF16),
            jax.ShapeDtypeStruct((bsz, heads, seq // TK_MLA, V_ROWS, TK_MLA), BF16),
        ],
        compiler_params=pltpu.CompilerParams(
            dimension_semantics=("arbitrary", "arbitrary"), vmem_limit_bytes=VMEM_LIMIT),
        name="mla_prep",
    )(planes, cos_t, sin_t, ctab, qg, kvg, wqt, wk, wvt)


def _mla_attn_kernel(qt_ref, k_ref, vt_ref, g_ref, o_ref, s_sc, m_sc, acc_sc, accp_sc):
    n_q, _, tq = qt_ref.shape
    n_kt = vt_ref.shape[0]
    n_parts = n_kt // STEPS_MLA
    part = tq // n_parts
    assert accp_sc.shape == (n_parts, V_ROWS, part)

    def finish(acc, row0):
        rows = pl.ds(pl.multiple_of(row0, part), part)
        o_t = acc[0:HEAD_DIM, :] / acc[HEAD_DIM:HEAD_DIM + 1, :]
        o_ref[rows, :] = (o_t.T * _silu(g_ref[rows, :])).astype(BF16)

    accp_sc[...] = jnp.ones(accp_sc.shape, F32)

    def scores(kt, qi):
        k = k_ref[pl.ds(pl.multiple_of(kt * TK_MLA, TK_MLA), TK_MLA), :]
        return jnp.dot(k, qt_ref[qi], preferred_element_type=F32)

    s_sc[0] = scores(0, 0)

    def q_tile(qi, carry):
        m_sc[...] = jnp.full(m_sc.shape, -jnp.inf, F32)
        acc_sc[...] = jnp.zeros(acc_sc.shape, F32)

        def step(kt, slot):
            wrap = jnp.asarray(kt + 1 == n_kt).astype(jnp.int32)
            s_sc[1 - slot] = scores((kt + 1) * (1 - wrap), jnp.minimum(qi + wrap, n_q - 1))
            s = s_sc[slot]
            m_prev = m_sc[...]
            m_new = jnp.maximum(m_prev, jnp.max(s, axis=0, keepdims=True))
            alpha = jnp.exp2(m_prev - m_new)
            p = jnp.exp2(s - m_new)
            acc_sc[...] = alpha * acc_sc[...] + jnp.dot(vt_ref[kt], p.astype(BF16),
                                                        preferred_element_type=F32)
            m_sc[...] = m_new

        prev = jnp.maximum(qi - 1, 0)

        def body(i, c):
            for u in range(STEPS_MLA):
                step(STEPS_MLA * i + u, u % 2)
            finish(accp_sc[i], prev * tq + i * part)
            return c

        lax.fori_loop(0, n_parts, body, 0)
        for j in range(n_parts):
            accp_sc[j] = acc_sc[:, j * part:(j + 1) * part]
        return carry

    lax.fori_loop(0, n_q, q_tile, 0)
    for j in range(n_parts):
        finish(accp_sc[j], (n_q - 1) * tq + j * part)


def _mla_attn(qt, k, vt, planes, bsz, seq, heads):
    n_q, tq = qt.shape[2], qt.shape[4]
    n_kt = seq // TK_MLA
    return pl.pallas_call(
        _mla_attn_kernel,
        grid=(bsz, heads),
        in_specs=[
            pl.BlockSpec((None, None, n_q, QK_PAD, tq), lambda b, h: (b, h, 0, 0, 0)),
            pl.BlockSpec((None, None, seq, QK_PAD), lambda b, h: (b, h, 0, 0)),
            pl.BlockSpec((None, None, n_kt, V_ROWS, TK_MLA), lambda b, h: (b, h, 0, 0, 0)),
            pl.BlockSpec((None, seq, HEAD_DIM), lambda b, h: (1, b, h)),
        ],
        out_specs=pl.BlockSpec((seq, HEAD_DIM), lambda b, h: (b, h)),
        out_shape=jax.ShapeDtypeStruct((bsz * seq, heads * HEAD_DIM), BF16),
        scratch_shapes=[pltpu.VMEM((2, TK_MLA, tq), F32), pltpu.VMEM((1, tq), F32),
                        pltpu.VMEM((V_ROWS, tq), F32),
                        pltpu.VMEM((n_kt // STEPS_MLA, V_ROWS, tq * STEPS_MLA // n_kt), F32)],
        compiler_params=pltpu.CompilerParams(
            dimension_semantics=("arbitrary", "arbitrary"), vmem_limit_bytes=VMEM_LIMIT),
        name="mla_attn",
    )(qt, k, vt, planes)


def _regroup(src_ref, dst_sc, stage_sc, seq):
    chunk = 256
    assert [d for _, d in PATTERNS] == [1, 4, 16]

    def copy1(c, carry):
        rows = pl.ds(pl.multiple_of(chunk * c, chunk), chunk)
        dst_sc[0, rows, :] = src_ref[rows, :].astype(BF16)
        return carry
    lax.fori_loop(0, seq // chunk, copy1, 0)

    len4, len16 = seq // 4, seq // 16
    for r in range(4):
        def copy4(c, carry, r=r):
            dst = pl.ds(pl.multiple_of(r * len4 + chunk * c, chunk), chunk)
            rows = src_ref[pl.ds(r + 4 * chunk * c, chunk, stride=4), :]
            stage_sc[dst, :] = rows
            dst_sc[1, dst, :] = rows.astype(BF16)
            return carry
        lax.fori_loop(0, len4 // chunk, copy4, 0)

    for r in range(4):
        for q in range(4):
            def copy16(c, carry, r=r, q=q):
                dst = pl.ds(pl.multiple_of((r + 4 * q) * len16 + chunk * c, chunk), chunk)
                rows = stage_sc[pl.ds(r * len4 + q + 4 * chunk * c, chunk, stride=4), :]
                dst_sc[2, dst, :] = rows.astype(BF16)
                return carry
            lax.fori_loop(0, len16 // chunk, copy16, 0)


def _dilated_kernel(q_ref, k_ref, v_ref, g_ref, tbl_ref, o_ref,
                    kc_sc, vc_sc, stage_sc, o_sc, m_sc, l_sc, *, seq, scale):
    tq = q_ref.shape[0]
    step = pl.program_id(2)

    @pl.when(step == 0)
    def _():
        _regroup(k_ref, kc_sc, stage_sc, seq)
        _regroup(v_ref, vc_sc, stage_sc, seq)

    t0 = step * tq
    n_units = tq // SUB
    for g, (_, d) in enumerate(PATTERNS):
        cls_len = seq // d
        sb_bits = (n_units // d).bit_length() - 1

        def unit(u, carry, g=g, d=d, cls_len=cls_len, sb_bits=sb_bits):
            r = lax.shift_right_logical(u, sb_bits)
            sb = u & ((1 << sb_bits) - 1)
            row0 = r + d * SUB * sb
            rows = pl.ds(row0, SUB, stride=d) if d > 1 else pl.ds(pl.multiple_of(row0, SUB), SUB)
            qc = (q_ref[rows, :] * scale).astype(BF16)
            i0 = t0 // d + sb * SUB
            ws = jnp.clip(i0 - HALF_WIN, 0, cls_len - WIN)
            var = (i0 - ws) // HALF_WIN
            win = pl.ds(pl.multiple_of(r * cls_len + ws, HALF_WIN), WIN)
            s = _nt_dot(qc, kc_sc[g, win, :]) + tbl_ref[g, var]
            m = jnp.max(s, axis=-1, keepdims=True)
            p = jnp.exp2(s - m)
            v_ext = jnp.concatenate([vc_sc[g, win, :], jnp.ones((WIN, HEAD_DIM), BF16)], axis=1)
            o_ext = jnp.dot(p.astype(BF16), v_ext, preferred_element_type=F32)
            o_sc[g, rows, :] = o_ext[:, :HEAD_DIM]
            l_sc[g, rows, :] = o_ext[:, HEAD_DIM:]
            m_sc[g, rows, :] = jnp.broadcast_to(m, (SUB, HEAD_DIM))
            return carry

        lax.fori_loop(0, n_units, unit, 0, unroll=UNROLL_DIL)

    m_all = jnp.maximum(jnp.maximum(m_sc[0], m_sc[1]), m_sc[2])
    num = jnp.zeros((tq, HEAD_DIM), F32)
    tot = jnp.zeros((tq, HEAD_DIM), F32)
    for g in range(len(PATTERNS)):
        w = jnp.exp2(m_sc[g] - m_all)
        num = num + w * o_sc[g]
        tot = tot + w * l_sc[g]
    o_ref[...] = ((num / tot) * _silu(g_ref[...])).astype(BF16)


def _dilated_attn(planes, tbl, bsz, seq, heads):
    tq = TQ_DIL
    per_b = seq // tq
    n_pat = len(PATTERNS)
    kern = functools.partial(_dilated_kernel, seq=seq, scale=LOG2E * HEAD_DIM ** -0.5)
    return pl.pallas_call(
        kern,
        grid=(bsz, heads, per_b),
        in_specs=[
            pl.BlockSpec((None, tq, HEAD_DIM), lambda b, h, i: (2, b * per_b + i, h)),
            pl.BlockSpec((None, seq, HEAD_DIM), lambda b, h, i: (3, b, h)),
            pl.BlockSpec((None, seq, HEAD_DIM), lambda b, h, i: (4, b, h)),
            pl.BlockSpec((None, tq, HEAD_DIM), lambda b, h, i: (5, b * per_b + i, h)),
            pl.BlockSpec((None, n_pat, 3, SUB, WIN), lambda b, h, i: (h, 0, 0, 0, 0)),
        ],
        out_specs=pl.BlockSpec((tq, HEAD_DIM), lambda b, h, i: (b * per_b + i, h)),
        out_shape=jax.ShapeDtypeStruct((bsz * seq, heads * HEAD_DIM), BF16),
        scratch_shapes=[
            pltpu.VMEM((n_pat, seq, HEAD_DIM), BF16),
            pltpu.VMEM((n_pat, seq, HEAD_DIM), BF16),
            pltpu.VMEM((seq, HEAD_DIM), F32),
            pltpu.VMEM((n_pat, tq, HEAD_DIM), F32),
            pltpu.VMEM((n_pat, tq, HEAD_DIM), F32),
            pltpu.VMEM((n_pat, tq, HEAD_DIM), F32),
        ],
        compiler_params=pltpu.CompilerParams(
            dimension_semantics=("arbitrary", "arbitrary", "arbitrary"),
            vmem_limit_bytes=VMEM_LIMIT),
        name="dilated_attn",
    )(planes, planes, planes, planes, tbl)


def _out_proj_kernel(ya_ref, yb_ref, x_ref, mod_ref, w_ref, fg_ref, o_ref, *, final):
    half = ya_ref.shape[1]
    y = (jnp.dot(ya_ref[...], w_ref[0:half, :], preferred_element_type=F32)
         + jnp.dot(yb_ref[...], w_ref[half:, :], preferred_element_type=F32))
    xn = x_ref[...] + mod_ref[2:3, :] * y
    if final:
        xn = _rms(xn, fg_ref[...])
    o_ref[...] = xn


def _out_proj(ya, yb, x2, mod, w_out, fg, layer, seq, final, in_place):
    rows, d = x2.shape
    tm = TM_OUT
    per_b = seq // tm
    half = ya.shape[1]
    kern = functools.partial(_out_proj_kernel, final=final)
    return pl.pallas_call(
        kern,
        grid=(rows // tm,),
        in_specs=[
            pl.BlockSpec((tm, half), lambda i: (i, 0)),
            pl.BlockSpec((tm, half), lambda i: (i, 0)),
            pl.BlockSpec((tm, d), lambda i: (i, 0)),
            pl.BlockSpec((None, None, 3, d), lambda i: (layer, i // per_b, 0, 0)),
            pl.BlockSpec((None,) + w_out.shape[1:], lambda i: (layer, 0, 0)),
            pl.BlockSpec((1, d), lambda i: (0, 0)),
        ],
        out_specs=pl.BlockSpec((tm, d), lambda i: (i, 0)),
        out_shape=jax.ShapeDtypeStruct((rows, d), F32),
        input_output_aliases={2: 0} if in_place else {},
        compiler_params=pltpu.CompilerParams(
            dimension_semantics=("arbitrary",), vmem_limit_bytes=VMEM_LIMIT),
        name="out_proj",
    )(ya, yb, x2, mod, w_out, fg)


def _t5_buckets(rel):
    nb = NUM_BUCKETS // 2
    max_exact = nb // 2
    base = np.where(rel > 0, nb, 0)
    n = np.abs(rel)
    large = max_exact + (np.log(np.maximum(n, 1) / max_exact)
                         / math.log(MAX_DISTANCE / max_exact) * (nb - max_exact)).astype(np.int32)
    large = np.minimum(large, nb - 1)
    return (base + np.where(n < max_exact, n, large)).astype(np.int32)


def _bias_tables(rel_bias):
    heads = rel_bias.shape[1]
    n = WIN + SUB
    pos = np.arange(n)
    delta = np.where(pos < WIN, pos, pos - n)
    rows = []
    for _, d in PATTERNS:
        for var in range(3):
            j = delta - var * HALF_WIN
            band = np.abs(j) <= HALF_WIN
            rows.append(np.where(band, _t5_buckets(np.where(band, j, 0) * d), NUM_BUCKETS))
    idx = np.stack(rows).reshape(-1)
    ext = jnp.concatenate([rel_bias.astype(F32) * LOG2E, jnp.full((1, heads), NEG, F32)], axis=0)
    gen = jnp.take(ext, jnp.asarray(idx), axis=0).reshape(len(rows), n, heads).transpose(2, 0, 1)
    tab = jnp.tile(gen, (1, 1, SUB))[:, :, :SUB * (n - 1)].reshape(heads, len(rows), SUB, n - 1)
    return tab[..., :WIN].reshape(heads, len(PATTERNS), 3, SUB, WIN)


def _rope_tables(positions):
    inv = 1.0 / (ROPE_THETA ** (jnp.arange(0, ROPE_DIM, 2, dtype=F32) / ROPE_DIM))
    ang = positions.astype(F32)[..., None] * inv
    cos, sin = jnp.cos(ang), jnp.sin(ang)
    ctab = jnp.concatenate([cos, cos, -sin, sin], axis=-1)
    return cos.swapaxes(1, 2), sin.swapaxes(1, 2), ctab


def kernel(x, c, positions, norm_g, ada_w, ada_b, w_in, q_a_norm_g, w_q_up, kv_a_norm_g,
           w_kv_up, rel_bias, w_out, final_norm_g):
    bsz, seq, d = x.shape
    depth = w_in.shape[0]
    heads = w_q_up.shape[2] // (HEAD_DIM + ROPE_DIM)
    width = heads * HEAD_DIM

    cq_w = w_in[:, :, 0:Q_LORA]
    ckv_w = w_in[:, :, Q_LORA:Q_LORA + KV_LORA]
    kr_w = w_in[:, :, Q_LORA + KV_LORA:Q_LORA + KV_LORA + ROPE_DIM]
    k1, k2 = kr_w[..., :HALF_ROPE], kr_w[..., HALF_ROPE:]
    pad = jnp.zeros((depth, d, GROUP - Q_LORA - KV_LORA - 2 * ROPE_DIM), w_in.dtype)
    w_lat = jnp.concatenate([cq_w, ckv_w, k1, k2, k2, k1, pad], axis=-1).astype(BF16)
    w_rest = w_in[:, :, Q_LORA + KV_LORA + ROPE_DIM:].astype(BF16)

    wqt = w_q_up.swapaxes(1, 2).astype(BF16)
    wkv = w_kv_up.reshape(depth, KV_LORA, heads, 2 * HEAD_DIM)
    wk = wkv[..., :HEAD_DIM].reshape(depth, KV_LORA, width).astype(BF16)
    wvt = wkv[..., HEAD_DIM:].reshape(depth, KV_LORA, width).swapaxes(1, 2).astype(BF16)
    w_out_b = w_out.astype(BF16)

    cos_t, sin_t, ctab = _rope_tables(positions)
    tbl = _bias_tables(rel_bias)

    c8 = jnp.pad(c, ((0, 8 - bsz), (0, 0)))
    mod = _adaln(c8, ada_w, ada_b)[:, :bsz].reshape(depth, bsz, 3, d)

    x2 = x.reshape(bsz * seq, d)
    fg = final_norm_g.reshape(1, d)
    norm_g3 = norm_g.reshape(depth, 1, d)
    qg3 = q_a_norm_g.reshape(depth, 1, Q_LORA)
    kvg3 = kv_a_norm_g.reshape(depth, 1, KV_LORA)
    for l in range(depth):
        planes = _in_proj(x2, mod, norm_g3, w_lat, w_rest, l, seq)
        qt, k, vt = _mla_prep(planes, cos_t, sin_t, ctab, qg3, kvg3, wqt, wk, wvt, l,
                              bsz, seq, heads)
        ya = _mla_attn(qt, k, vt, planes, bsz, seq, heads)
        yb = _dilated_attn(planes, tbl, bsz, seq, heads)
        x2 = _out_proj(ya, yb, x2, mod, w_out_b, fg, l, seq, final=(l == depth - 1),
                       in_place=(l > 0))
    return x2.reshape(bsz, seq, d)
```
